```python
import numpy as np
import jax
import jax.numpy as jnp
from jax import lax

D_MODEL = 1024
BATCH = 8
SEQ = 2048
DEPTH = 4
DEC_BATCH = 128
DEC_SEQ = 8
PAST_LEN = 8192
PAGE_SIZE = 128

N_HEADS = 16
HEAD_DIM = D_MODEL // N_HEADS
N_KV_HEADS = 4
GROUP = N_HEADS // N_KV_HEADS
Q_DIM = N_HEADS * HEAD_DIM
KV_DIM = N_KV_HEADS * HEAD_DIM
QKV_DIM = Q_DIM + 2 * KV_DIM
WINDOW = 128
ROPE_THETA = 500000.0
ROT_DIM = HEAD_DIM // 4
D_FF = 11 * D_MODEL // 4
UP_DIM = 2 * D_FF
CONV_WIDTH = 3
Q_BLOCK = 128
N_MOD = 6
N_SB_LAYERS = (DEPTH + 1) // 2
N_SWA_LAYERS = DEPTH // 2
RMS_EPS = 1e-6
SB_BIAS_INIT = -6.0

kernel_name = "stickbreak_swa_sink_convffn_hybrid_step"


def rms_norm(x, g):
    xf = x.astype(jnp.float32)
    y = xf * lax.rsqrt(jnp.mean(xf * xf, axis=-1, keepdims=True) + RMS_EPS)
    return (y * g.astype(jnp.float32)).astype(x.dtype)


def modulation(c, w, b):
    m = jax.nn.silu(c) @ w + b
    return [t[:, None, :] for t in jnp.split(m, N_MOD, axis=-1)]


def modulate(h, shift, scale):
    return h * (1.0 + scale) + shift


def project_qkv(h, w_qkv):
    B, T, _ = h.shape
    qkv = h @ w_qkv
    q = qkv[..., :Q_DIM].reshape(B, T, N_HEADS, HEAD_DIM)
    k = qkv[..., Q_DIM:Q_DIM + KV_DIM].reshape(B, T, N_KV_HEADS, HEAD_DIM)
    v = qkv[..., Q_DIM + KV_DIM:].reshape(B, T, N_KV_HEADS, HEAD_DIM)
    return q, k, v


def partial_rope(x, pos):
    half = ROT_DIM // 2
    inv_freq = jnp.power(jnp.float32(ROPE_THETA), -jnp.arange(half, dtype=jnp.float32) * (2.0 / ROT_DIM))
    ang = pos.astype(jnp.float32)[:, None] * inv_freq[None, :]
    cos = jnp.cos(ang)[:, None, :]
    sin = jnp.sin(ang)[:, None, :]
    xf = x.astype(jnp.float32)
    x1 = xf[..., :half]
    x2 = xf[..., half:ROT_DIM]
    out = jnp.concatenate([x1 * cos - x2 * sin, x2 * cos + x1 * sin, xf[..., ROT_DIM:]], axis=-1)
    return out.astype(x.dtype)


def stick_breaking_block(q, k_segs, v_segs, q_pos, k_pos, bias):
    qf = q.astype(jnp.float32) * (HEAD_DIM ** -0.5)
    z = jnp.concatenate(
        [jnp.einsum("btkgd,bskd->bkgts", qf, kk.astype(jnp.float32)) for kk in k_segs], axis=-1)
    z = z + bias.astype(jnp.float32).reshape(N_KV_HEADS, GROUP, 1, 1)
    mask = k_pos[None, :] < q_pos[:, None]
    log_beta = jax.nn.log_sigmoid(z)
    log_keep = jnp.where(mask, log_beta - z, 0.0)
    tail = lax.cumsum(log_keep, axis=z.ndim - 1, reverse=True) - log_keep
    a = jnp.where(mask, jnp.exp(log_beta + tail), 0.0)
    bounds = np.cumsum([kk.shape[1] for kk in k_segs])[:-1].tolist()
    a_segs = jnp.split(a, bounds, axis=-1)
    o = None
    for a_s, vv in zip(a_segs, v_segs):
        term = jnp.einsum("bkgts,bskd->btkgd", a_s, vv.astype(jnp.float32))
        o = term if o is None else o + term
    return o


def stick_breaking_mixer(h, w_qkv, w_o, bias, past_k=None, past_v=None):
    B, T, _ = h.shape
    P = 0 if past_k is None else past_k.shape[1]
    q, k, v = project_qkv(h, w_qkv)
    q = q.reshape(B, T, N_KV_HEADS, GROUP, HEAD_DIM)
    outs = []
    for start in range(0, T, Q_BLOCK):
        end = min(start + Q_BLOCK, T)
        k_segs = [k[:, :end]]
        v_segs = [v[:, :end]]
        if past_k is not None:
            k_segs = [past_k] + k_segs
            v_segs = [past_v] + v_segs
        q_pos = P + jnp.arange(start, end)
        k_pos = jnp.arange(P + end)
        outs.append(stick_breaking_block(q[:, start:end], k_segs, v_segs, q_pos, k_pos, bias))
    o = jnp.concatenate(outs, axis=1).reshape(B, T, Q_DIM).astype(h.dtype)
    return o @ w_o, k, v


def sink_window_attention(q, k, v, q_pos, k_pos, sinks):
    s = jnp.einsum("...tkgd,...skd->...kgts", q.astype(jnp.float32), k.astype(jnp.float32)) * (HEAD_DIM ** -0.5)
    dist = q_pos[..., :, None] - k_pos[..., None, :]
    mask = (dist >= 0) & (dist <= WINDOW) & (k_pos[..., None, :] >= 0)
    s = jnp.where(mask[..., None, None, :, :], s, -jnp.inf)
    sink_col = jnp.broadcast_to(sinks.astype(jnp.float32).reshape(N_KV_HEADS, GROUP, 1, 1), s.shape[:-1] + (1,))
    p = jax.nn.softmax(jnp.concatenate([s, sink_col], axis=-1), axis=-1)[..., :-1]
    return jnp.einsum("...kgts,...skd->...tkgd", p, v.astype(jnp.float32))


def swa_prompt_mixer(h, w_qkv, w_o, sinks):
    B, T, _ = h.shape
    pos = jnp.arange(T)
    q, k, v = project_qkv(h, w_qkv)
    q = partial_rope(q, pos)
    k = partial_rope(k, pos)
    nb = T // WINDOW
    qb = q.reshape(B, nb, WINDOW, N_KV_HEADS, GROUP, HEAD_DIM)

    def band(x):
        xb = x.reshape(B, nb, WINDOW, N_KV_HEADS, HEAD_DIM)
        prev = jnp.concatenate([jnp.zeros_like(xb[:, :1]), xb[:, :-1]], axis=1)
        return jnp.concatenate([prev, xb], axis=2)

    blk = jnp.arange(nb)[:, None] * WINDOW
    q_pos = blk + jnp.arange(WINDOW)[None, :]
    k_pos = blk - WINDOW + jnp.arange(2 * WINDOW)[None, :]
    o = sink_window_attention(qb, band(k), band(v), q_pos, k_pos, sinks)
    o = o.reshape(B, T, Q_DIM).astype(h.dtype)
    keep = min(WINDOW, T)
    return o @ w_o, k[:, T - keep:], v[:, T - keep:]


def swa_sample_mixer(h, w_qkv, w_o, sinks, buf_k, buf_v, past_len):
    B, T, _ = h.shape
    wb = buf_k.shape[1]
    pos = past_len + jnp.arange(T)
    q, k, v = project_qkv(h, w_qkv)
    q = partial_rope(q, pos).reshape(B, T, N_KV_HEADS, GROUP, HEAD_DIM)
    k = partial_rope(k, pos)
    k_all = jnp.concatenate([buf_k.astype(k.dtype), k], axis=1)
    v_all = jnp.concatenate([buf_v.astype(v.dtype), v], axis=1)
    k_pos = past_len - wb + jnp.arange(wb + T)
    o = sink_window_attention(q, k_all, v_all, pos, k_pos, sinks)
    o = o.reshape(B, T, Q_DIM).astype(h.dtype)
    return o @ w_o, k_all[:, -wb:], v_all[:, -wb:]


def conv_ffn(h, w_up, conv_w, conv_b, w_down, prev):
    T = h.shape[1]
    u = h @ w_up
    full = jnp.concatenate([prev.astype(u.dtype), u], axis=1)
    conv = conv_b
    for j in range(CONV_WIDTH):
        conv = conv + conv_w[j] * full[:, j:j + T]
    gate, val = jnp.split(conv, 2, axis=-1)
    y = (jax.nn.gelu(gate, approximate=True) * val) @ w_down
    return y, full[:, -(CONV_WIDTH - 1):]


def setup_inputs(seed: int = 0) -> dict:
    key = jax.random.key(seed)
    ks = jax.random.split(key, 26)
    n_pages = PAST_LEN // PAGE_SIZE
    n_used = DEC_BATCH * n_pages
    n_phys = n_used + n_used // 4
    wb = min(WINDOW, PAST_LEN)
    f32 = jnp.float32

    def nrm(k, shape, scale=1.0):
        return scale * jax.random.normal(k, shape, f32)

    page_table = jax.random.permutation(ks[7], n_phys)[:n_used].reshape(DEC_BATCH, n_pages).astype(jnp.int32)
    return {
        "x_prompt": nrm(ks[0], (BATCH, SEQ, D_MODEL)),
        "x_sample": nrm(ks[1], (DEC_BATCH, DEC_SEQ, D_MODEL)),
        "cache_sb_k": nrm(ks[2], (N_SB_LAYERS, n_phys, PAGE_SIZE, N_KV_HEADS, HEAD_DIM)),
        "cache_sb_v": nrm(ks[3], (N_SB_LAYERS, n_phys, PAGE_SIZE, N_KV_HEADS, HEAD_DIM)),
        "state_swa_k": nrm(ks[4], (N_SWA_LAYERS, DEC_BATCH, wb, N_KV_HEADS, HEAD_DIM)),
        "state_swa_v": nrm(ks[5], (N_SWA_LAYERS, DEC_BATCH, wb, N_KV_HEADS, HEAD_DIM)),
        "state_conv": nrm(ks[6], (DEPTH, DEC_BATCH, CONV_WIDTH - 1, UP_DIM)),
        "page_table": page_table,
        "c_prompt": nrm(ks[8], (BATCH, D_MODEL)),
        "c_sample": nrm(ks[9], (DEC_BATCH, D_MODEL)),
        "mod_w": nrm(ks[10], (DEPTH, D_MODEL, N_MOD * D_MODEL), D_MODEL ** -0.5),
        "mod_b": nrm(ks[11], (DEPTH, N_MOD * D_MODEL), 0.01),
        "norm_mix_pre": 1.0 + nrm(ks[12], (DEPTH, D_MODEL), 0.05),
        "norm_mix_post": 1.0 + nrm(ks[13], (DEPTH, D_MODEL), 0.05),
        "norm_ffn_pre": 1.0 + nrm(ks[14], (DEPTH, D_MODEL), 0.05),
        "norm_ffn_post": 1.0 + nrm(ks[15], (DEPTH, D_MODEL), 0.05),
        "sb_w_qkv": nrm(ks[16], (N_SB_LAYERS, D_MODEL, QKV_DIM), D_MODEL ** -0.5),
        "sb_w_o": nrm(ks[17], (N_SB_LAYERS, Q_DIM, D_MODEL), Q_DIM ** -0.5),
        "sb_bias": SB_BIAS_INIT + nrm(ks[25], (N_SB_LAYERS, N_HEADS), 0.1),
        "swa_w_qkv": nrm(ks[18], (N_SWA_LAYERS, D_MODEL, QKV_DIM), D_MODEL ** -0.5),
        "swa_w_o": nrm(ks[19], (N_SWA_LAYERS, Q_DIM, D_MODEL), Q_DIM ** -0.5),
        "swa_sinks": nrm(ks[20], (N_SWA_LAYERS, N_HEADS)),
        "ffn_w_up": nrm(ks[21], (DEPTH, D_MODEL, UP_DIM), D_MODEL ** -0.5),
        "ffn_conv_w": nrm(ks[22], (DEPTH, CONV_WIDTH, UP_DIM), CONV_WIDTH ** -0.5),
        "ffn_conv_b": nrm(ks[23], (DEPTH, UP_DIM), 0.01),
        "ffn_w_down": nrm(ks[24], (DEPTH, D_FF, D_MODEL), D_FF ** -0.5),
    }


def reference(x_prompt, x_sample, cache_sb_k, cache_sb_v, state_swa_k, state_swa_v, state_conv, page_table,
              c_prompt, c_sample, mod_w, mod_b, norm_mix_pre, norm_mix_post, norm_ffn_pre, norm_ffn_post,
              sb_w_qkv, sb_w_o, sb_bias, swa_w_qkv, swa_w_o, swa_sinks, ffn_w_up, ffn_conv_w, ffn_conv_b,
              ffn_w_down):
    dec_batch, n_pages = page_table.shape
    past_len = n_pages * PAGE_SIZE
    bp = x_prompt.shape[0]
    xp, xs = x_prompt, x_sample
    sb_k_p, sb_v_p, sb_k_s, sb_v_s = [], [], [], []
    swa_k_p, swa_v_p, swa_k_s, swa_v_s = [], [], [], []
    conv_p, conv_s = [], []
    for i in range(DEPTH):
        j = i // 2
        mp = modulation(c_prompt, mod_w[i], mod_b[i])
        ms = modulation(c_sample, mod_w[i], mod_b[i])
        hp = modulate(rms_norm(xp, norm_mix_pre[i]), mp[0], mp[1])
        hs = modulate(rms_norm(xs, norm_mix_pre[i]), ms[0], ms[1])
        if i % 2 == 0:
            yp, kp, vp = stick_breaking_mixer(hp, sb_w_qkv[j], sb_w_o[j], sb_bias[j])
            past_k = cache_sb_k[j][page_table].reshape(dec_batch, past_len, N_KV_HEADS, HEAD_DIM)
            past_v = cache_sb_v[j][page_table].reshape(dec_batch, past_len, N_KV_HEADS, HEAD_DIM)
            ys, ks_, vs_ = stick_breaking_mixer(hs, sb_w_qkv[j], sb_w_o[j], sb_bias[j], past_k, past_v)
            sb_k_p.append(kp)
            sb_v_p.append(vp)
            sb_k_s.append(ks_)
            sb_v_s.append(vs_)
        else:
            yp, kp, vp = swa_prompt_mixer(hp, swa_w_qkv[j], swa_w_o[j], swa_sinks[j])
            ys, ks_, vs_ = swa_sample_mixer(hs, swa_w_qkv[j], swa_w_o[j], swa_sinks[j],
                                            state_swa_k[j], state_swa_v[j], past_len)
            swa_k_p.append(kp)
            swa_v_p.append(vp)
            swa_k_s.append(ks_)
            swa_v_s.append(vs_)
        xp = xp + mp[2] * rms_norm(yp, norm_mix_post[i])
        xs = xs + ms[2] * rms_norm(ys, norm_mix_post[i])
        hp = modulate(rms_norm(xp, norm_ffn_pre[i]), mp[3], mp[4])
        hs = modulate(rms_norm(xs, norm_ffn_pre[i]), ms[3], ms[4])
        zero_prev = jnp.zeros((bp, CONV_WIDTH - 1, UP_DIM), xp.dtype)
        fp, cp = conv_ffn(hp, ffn_w_up[i], ffn_conv_w[i], ffn_conv_b[i], ffn_w_down[i], zero_prev)
        fs, cs = conv_ffn(hs, ffn_w_up[i], ffn_conv_w[i], ffn_conv_b[i], ffn_w_down[i], state_conv[i])
        conv_p.append(cp)
        conv_s.append(cs)
        xp = xp + mp[5] * rms_norm(fp, norm_ffn_post[i])
        xs = xs + ms[5] * rms_norm(fs, norm_ffn_post[i])
    return (xp, xs,
            jnp.stack(sb_k_p), jnp.stack(sb_v_p), jnp.stack(sb_k_s), jnp.stack(sb_v_s),
            jnp.stack(swa_k_p), jnp.stack(swa_v_p), jnp.stack(swa_k_s), jnp.stack(swa_v_s),
            jnp.stack(conv_p), jnp.stack(conv_s))
```

```python
import functools

import numpy as np
import jax
import jax.numpy as jnp
from jax import lax
from jax.experimental import pallas as pl
from jax.experimental.pallas import tpu as pltpu

F32 = jnp.float32
BF16 = jnp.bfloat16

HEAD_DIM = 64
N_HEADS = 16
N_KV_HEADS = 4
GROUP = N_HEADS // N_KV_HEADS
Q_DIM = N_HEADS * HEAD_DIM
KV_DIM = N_KV_HEADS * HEAD_DIM
PAGE_SIZE = 128
WINDOW = 128
Q_BLOCK = 128
ROT_DIM = HEAD_DIM // 4
ROT_HALF = ROT_DIM // 2
ROPE_THETA = 500000.0
RMS_EPS = 1e-6
N_MOD = 6
LANES = 128
N_QBLK = Q_DIM // LANES
N_PAIR = KV_DIM // LANES
VMEM_LIMIT = 56 * 1024 * 1024

HEAD_PERM = [4 * (2 * (i // 4) + p) + (i % 4) for i in range(N_QBLK) for p in range(2)]


def _cparams(sem):
    return pltpu.CompilerParams(dimension_semantics=sem, vmem_limit_bytes=VMEM_LIMIT)


def _rms(x, g):
    return x * lax.rsqrt(jnp.mean(x * x, axis=-1, keepdims=True) + RMS_EPS) * g


def _rows(m, rows):
    mr = m.shape[0]
    if mr == 1 or mr == rows:
        return m
    return jnp.concatenate([m] * (rows // mr), axis=0)


def _norm_mod(x, g, shift, scale):
    rows = x.shape[0]
    return _rms(x, g) * (1.0 + _rows(scale, rows)) + _rows(shift, rows)


def _mod_kernel(c_ref, w_ref, b_ref, o_ref):
    c = c_ref[...]
    s = c / (1.0 + jnp.exp(-c))
    o_ref[...] = jnp.dot(s.astype(BF16), w_ref[...].astype(BF16), preferred_element_type=F32) + b_ref[...]


def _modulation(c_all, mod_w, mod_b):
    depth, d, _ = mod_w.shape
    n = c_all.shape[0]
    return pl.pallas_call(
        _mod_kernel,
        grid=(depth, N_MOD),
        in_specs=[
            pl.BlockSpec((n, d), lambda i, k: (0, 0)),
            pl.BlockSpec((None, d, d), lambda i, k: (i, 0, k)),
            pl.BlockSpec((None, None, 1, d), lambda i, k: (i, k, 0, 0)),
        ],
        out_specs=pl.BlockSpec((None, None, n, d), lambda i, k: (i, k, 0, 0)),
        out_shape=jax.ShapeDtypeStruct((depth, N_MOD, n, d), F32),
        compiler_params=_cparams(("arbitrary", "arbitrary")),
        name="modulation",
    )(c_all, mod_w, mod_b.reshape(depth, N_MOD, 1, d))


def _qkv_kernel(*refs, rope, kc):
    if rope:
        (x_ref, sh_ref, sc_ref, g_ref, wq_ref, wkv_ref, cq_ref, s1_ref, s2_ref, ck_ref, sk_ref,
         q_ref, kt_ref, vt_ref, kvb_ref) = refs
    else:
        x_ref, sh_ref, sc_ref, g_ref, wq_ref, wkv_ref, q_ref, kt_ref, vt_ref, kvb_ref = refs
    tm = x_ref.shape[0]
    h = _norm_mod(x_ref[...], g_ref[...], sh_ref[...], sc_ref[...]).astype(BF16)
    q = jnp.dot(h, wq_ref[...], preferred_element_type=F32)
    kvt = lax.dot_general(wkv_ref[...], h, (((1,), (1,)), ((), ())),
                          preferred_element_type=F32)
    scale = HEAD_DIM ** -0.5
    if rope:
        cq, s1, s2 = cq_ref[...], s1_ref[...], s2_ref[...]
        for i in range(N_QBLK):
            xr = q[:, LANES * i:LANES * (i + 1)]
            rot = xr * cq + pltpu.roll(xr, ROT_HALF, 1) * s1 + pltpu.roll(xr, LANES - ROT_HALF, 1) * s2
            q_ref[:, LANES * i:LANES * (i + 1)] = (rot * scale).astype(BF16)
        ck, sk = ck_ref[...], sk_ref[...]
        pieces = []
        for c in range(N_KV_HEADS):
            base = HEAD_DIM * c
            x1 = kvt[base:base + ROT_HALF]
            x2 = kvt[base + ROT_HALF:base + ROT_DIM]
            pieces += [x1 * ck - x2 * sk, x2 * ck + x1 * sk, kvt[base + ROT_DIM:base + HEAD_DIM]]
        kt = jnp.concatenate(pieces, axis=0)
    else:
        q_ref[...] = (q * scale).astype(BF16)
        kt = kvt[:KV_DIM]
    vt = kvt[KV_DIM:]
    kt_ref[...] = kt
    vt_ref[...] = vt
    kb = kt.astype(BF16)
    vb = vt.astype(BF16)
    for c in range(tm // kc):
        kvb_ref[c, :KV_DIM, :] = kb[:, kc * c:kc * (c + 1)]
        kvb_ref[c, KV_DIM:, :] = vb[:, kc * c:kc * (c + 1)]


def _qkv(x, shift, scale, g, wq, wkvt, rope_tabs, *, tm, mod_map):
    nb, r, d = x.shape
    kc = min(LANES, tm)
    nt = r // tm
    rope = rope_tabs is not None
    mr = shift.shape[1]
    in_specs = [
        pl.BlockSpec((None, tm, d), lambda b, t: (b, t, 0)),
        pl.BlockSpec((None, mr, d), lambda b, t: (mod_map(b), 0, 0)),
        pl.BlockSpec((None, mr, d), lambda b, t: (mod_map(b), 0, 0)),
        pl.BlockSpec((1, d), lambda b, t: (0, 0)),
        pl.BlockSpec(wq.shape, lambda b, t: (0, 0)),
        pl.BlockSpec(wkvt.shape, lambda b, t: (0, 0)),
    ]
    args = [x, shift, scale, g, wq, wkvt]
    if rope:
        cq, s1, s2, ck, sk, per_b = rope_tabs
        if per_b:
            qspec = pl.BlockSpec((None, 1, LANES), lambda b, t: (b, 0, 0))
            kspec = pl.BlockSpec((None, ROT_HALF, tm), lambda b, t: (b, 0, 0))
        else:
            qspec = pl.BlockSpec((None, tm, LANES), lambda b, t: (0, t, 0))
            kspec = pl.BlockSpec((None, ROT_HALF, tm), lambda b, t: (0, 0, t))
        in_specs += [qspec, qspec, qspec, kspec, kspec]
        args += [cq, s1, s2, ck, sk]
    out_shape = [
        jax.ShapeDtypeStruct((nb, r, Q_DIM), BF16),
        jax.ShapeDtypeStruct((nb, KV_DIM, r), F32),
        jax.ShapeDtypeStruct((nb, KV_DIM, r), F32),
        jax.ShapeDtypeStruct((nb, r // kc, 2 * KV_DIM, kc), BF16),
    ]
    out_specs = [
        pl.BlockSpec((None, tm, Q_DIM), lambda b, t: (b, t, 0)),
        pl.BlockSpec((None, KV_DIM, tm), lambda b, t: (b, 0, t)),
        pl.BlockSpec((None, KV_DIM, tm), lambda b, t: (b, 0, t)),
        pl.BlockSpec((None, tm // kc, 2 * KV_DIM, kc), lambda b, t: (b, t, 0, 0)),
    ]
    return pl.pallas_call(
        functools.partial(_qkv_kernel, rope=rope, kc=kc),
        grid=(nb, nt),
        in_specs=in_specs,
        out_specs=out_specs,
        out_shape=out_shape,
        compiler_params=_cparams(("arbitrary", "arbitrary")),
        name="qkv_rope" if rope else "qkv",
    )(*args)


LOG2E = 1.4426950408889634


def _sb_split(z, mask):
    n = z.shape[1] // LANES
    sp = jnp.maximum(z, 0.0) + jnp.log(1.0 + jnp.exp2(jnp.abs(z) * (-LOG2E)))
    if mask is not None:
        sp = jnp.where(mask, sp, 0.0)
    hi_f = lax.bitcast_convert_type(
        lax.bitcast_convert_type(sp, jnp.uint32) & jnp.uint32(0xFFFF0000), F32)
    hi = hi_f.astype(BF16)
    lo = (sp - hi_f).astype(BF16)
    return jnp.concatenate(
        [jnp.concatenate([hi[:, LANES * k:LANES * (k + 1)], lo[:, LANES * k:LANES * (k + 1)]], axis=1)
         for k in range(n)], axis=0)


def _sb_finish(z, t, mask, carry):
    r = z.shape[0]
    n = z.shape[1] // LANES
    tails = [None] * n
    for k in reversed(range(n)):
        blk = t[r * k:r * (k + 1)]
        tails[k] = blk[:, :LANES] + carry
        carry = carry + blk[:, LANES:]
    tail = tails[0] if n == 1 else jnp.concatenate(tails, axis=1)
    a = jnp.exp(z + tail)
    if mask is not None:
        a = jnp.where(mask, a, 0.0)
    return a.astype(BF16), carry


def _cumsum_total_matrix():
    j = np.arange(LANES)[:, None]
    c = np.arange(LANES)[None, :]
    half = np.concatenate([-(j >= c).astype(np.float32), -np.ones((LANES, LANES), np.float32)], axis=1)
    return jnp.asarray(np.concatenate([half, half], axis=0), dtype=BF16)


def _sb_prompt_kernel(bias_ref, q_ref, kvb_ref, w_ref, o_ref,
                      qs0_ref, qs1_ref, c0_ref, c1_ref, acc0_ref, acc1_ref):
    qi = pl.program_id(1)
    rows = GROUP * Q_BLOCK
    lane = lax.broadcasted_iota(jnp.int32, (Q_BLOCK, LANES), 1)
    top = qi // 2
    qs_refs, c_refs, acc_refs = (qs0_ref, qs1_ref), (c0_ref, c1_ref), (acc0_ref, acc1_ref)
    nt_dims = (((1,), (1,)), ((), ()))
    for j in range(N_PAIR):
        for p in range(2):
            in_half = (lane // HEAD_DIM) == p
            for g in range(GROUP):
                blk = q_ref[:, LANES * (GROUP * j + g):LANES * (GROUP * j + g + 1)]
                qs_refs[p][Q_BLOCK * g:Q_BLOCK * (g + 1), :] = jnp.where(in_half, blk, 0)

        def chunk(ci, masked):
            kt = jnp.concatenate([kvb_ref[2 * ci, LANES * j:LANES * (j + 1), :],
                                  kvb_ref[2 * ci + 1, LANES * j:LANES * (j + 1), :]], axis=1)
            vt = jnp.concatenate([kvb_ref[2 * ci, KV_DIM + LANES * j:KV_DIM + LANES * (j + 1), :],
                                  kvb_ref[2 * ci + 1, KV_DIM + LANES * j:KV_DIM + LANES * (j + 1), :]],
                                 axis=1)
            mask = None
            if masked:
                r_idx = lax.broadcasted_iota(jnp.int32, (rows, 2 * LANES), 0) & (Q_BLOCK - 1)
                c_idx = lax.broadcasted_iota(jnp.int32, (rows, 2 * LANES), 1)
                mask = (ci * 2 * LANES + c_idx) < (qi * Q_BLOCK + r_idx)
            zs = [jnp.dot(qs_refs[p][...], kt, preferred_element_type=F32) for p in range(2)]
            zs = [jnp.concatenate(
                [zs[p][Q_BLOCK * g:Q_BLOCK * (g + 1)] + bias_ref[2 * (GROUP * j + g) + p]
                 for g in range(GROUP)], axis=0) for p in range(2)]
            hls = [_sb_split(zs[p], mask) for p in range(2)]
            ts = [jnp.dot(hls[p], w_ref[...], preferred_element_type=F32) for p in range(2)]
            aw = [_sb_finish(zs[p], ts[p], mask, c_refs[p][...]) for p in range(2)]
            for p in range(2):
                c_refs[p][...] = aw[p][1]
            pv = [lax.dot_general(aw[p][0], vt, nt_dims, preferred_element_type=F32) for p in range(2)]
            for p in range(2):
                acc_refs[p][...] += pv[p]

        for p in range(2):
            c_refs[p][...] = jnp.zeros_like(c_refs[p])
            acc_refs[p][...] = jnp.zeros_like(acc_refs[p])
        chunk(top, True)

        def body(n, _):
            chunk(top - 1 - n, False)
            return 0

        lax.fori_loop(0, top, body, 0)
        lane_r = lax.broadcasted_iota(jnp.int32, (rows, LANES), 1)
        o_pair = jnp.where((lane_r // HEAD_DIM) == 0, acc0_ref[...], acc1_ref[...]).astype(BF16)
        for g in range(GROUP):
            o_ref[:, LANES * (GROUP * j + g):LANES * (GROUP * j + g + 1)] = \
                o_pair[Q_BLOCK * g:Q_BLOCK * (g + 1)]


def _sb_prompt(q, kvb, bias_perm):
    b, t, _ = q.shape
    nq = t // Q_BLOCK
    nck = kvb.shape[1]
    rows = GROUP * Q_BLOCK
    return pl.pallas_call(
        _sb_prompt_kernel,
        grid_spec=pltpu.PrefetchScalarGridSpec(
            num_scalar_prefetch=1,
            grid=(b, nq),
            in_specs=[
                pl.BlockSpec((None, Q_BLOCK, Q_DIM), lambda bi, qi, s: (bi, qi, 0)),
                pl.BlockSpec((None, nck, 2 * KV_DIM, LANES), lambda bi, qi, s: (bi, 0, 0, 0)),
                pl.BlockSpec((2 * LANES, 2 * LANES), lambda bi, qi, s: (0, 0)),
            ],
            out_specs=pl.BlockSpec((None, Q_BLOCK, Q_DIM), lambda bi, qi, s: (bi, qi, 0)),
            scratch_shapes=[pltpu.VMEM((rows, LANES), BF16), pltpu.VMEM((rows, LANES), BF16),
                            pltpu.VMEM((rows, LANES), F32), pltpu.VMEM((rows, LANES), F32),
                            pltpu.VMEM((rows, LANES), F32), pltpu.VMEM((rows, LANES), F32)],
        ),
        out_shape=jax.ShapeDtypeStruct((b, t, Q_DIM), BF16),
        compiler_params=_cparams(("arbitrary", "arbitrary")),
        name="sb_prompt",
    )(bias_perm, q, kvb, _cumsum_total_matrix())


def _sb_sample_kernel(*refs, n_pp, dt):
    qe_ref, bias_ref, w_ref, kn_ref, vn_ref = refs[1:6]
    k_refs = refs[6:6 + n_pp]
    v_refs = refs[6 + n_pp:6 + 2 * n_pp]
    o_ref, c_ref, acc_ref = refs[6 + 2 * n_pp:]
    s = pl.program_id(1)
    qe = qe_ref[...]
    bias = bias_ref[...]
    r = qe.shape[0]
    nt_dims = (((1,), (1,)), ((), ()))

    @pl.when(s == 0)
    def _():
        t_row = lax.broadcasted_iota(jnp.int32, (r, LANES), 0) % dt
        t_key = lax.broadcasted_iota(jnp.int32, (r, LANES), 1)
        z = jnp.dot(qe, kn_ref[...], preferred_element_type=F32) + bias
        mask = t_key < t_row
        t = jnp.dot(_sb_split(z, mask), w_ref[...], preferred_element_type=F32)
        a, c_new = _sb_finish(z, t, mask, jnp.zeros((r, LANES), F32))
        c_ref[...] = c_new
        acc_ref[...] = lax.dot_general(a, vn_ref[...], nt_dims, preferred_element_type=F32)

    z = jnp.concatenate(
        [jnp.dot(qe, k_refs[k][...].astype(BF16), preferred_element_type=F32) + bias for k in range(n_pp)],
        axis=1)
    hl = _sb_split(z, None)
    hr = hl.shape[0] // 2 if n_pp % 2 == 0 else hl.shape[0]
    t = jnp.concatenate([jnp.dot(hl[i:i + hr], w_ref[...], preferred_element_type=F32)
                         for i in range(0, hl.shape[0], hr)], axis=0)
    a, c_new = _sb_finish(z, t, None, c_ref[...])
    c_ref[...] = c_new
    hk = (n_pp // 2 if n_pp % 2 == 0 else n_pp)
    pv = None
    for k0 in range(0, n_pp, hk):
        vt = jnp.concatenate([v_refs[k][...].astype(BF16) for k in range(k0, k0 + hk)], axis=1)
        part = lax.dot_general(a[:, PAGE_SIZE * k0:PAGE_SIZE * (k0 + hk)], vt, nt_dims,
                               preferred_element_type=F32)
        pv = part if pv is None else pv + part
    acc_ref[...] += pv

    @pl.when(s == pl.num_programs(1) - 1)
    def _():
        o_ref[...] = acc_ref[...]


def _sb_sample(page_table, qe, bias_rows, knew, vnew, cache_kt, cache_vt, layer, dt):
    db, r, _ = qe.shape
    n_pages = page_table.shape[1]
    n_pp = min(16, n_pages)
    while n_pages % n_pp:
        n_pp -= 1
    steps = n_pages // n_pp

    def page_spec(k):
        return pl.BlockSpec((None, None, KV_DIM, PAGE_SIZE),
                            lambda b, s, pt: (layer, pt[b, n_pages - n_pp * (s + 1) + k], 0, 0))

    in_specs = [
        pl.BlockSpec((None, r, KV_DIM), lambda b, s, pt: (b, 0, 0)),
        pl.BlockSpec((r, LANES), lambda b, s, pt: (0, 0)),
        pl.BlockSpec((2 * LANES, 2 * LANES), lambda b, s, pt: (0, 0)),
        pl.BlockSpec((None, KV_DIM, LANES), lambda b, s, pt: (b, 0, 0)),
        pl.BlockSpec((None, KV_DIM, LANES), lambda b, s, pt: (b, 0, 0)),
    ] + [page_spec(k) for k in range(n_pp)] * 2
    return pl.pallas_call(
        functools.partial(_sb_sample_kernel, n_pp=n_pp, dt=dt),
        grid_spec=pltpu.PrefetchScalarGridSpec(
            num_scalar_prefetch=1,
            grid=(db, steps),
            in_specs=in_specs,
            out_specs=pl.BlockSpec((None, r, KV_DIM), lambda b, s, pt: (b, 0, 0)),
            scratch_shapes=[pltpu.VMEM((r, LANES), F32), pltpu.VMEM((r, KV_DIM), F32)],
        ),
        out_shape=jax.ShapeDtypeStruct((db, r, KV_DIM), F32),
        compiler_params=_cparams(("arbitrary", "arbitrary")),
        name="sb_sample",
    )(page_table, qe, bias_rows, _cumsum_total_matrix(), knew, vnew,
      *([cache_kt] * n_pp), *([cache_vt] * n_pp))


def _swa_prompt_kernel(sink_ref, q_ref, kvb_ref, o_ref, half_ref):
    qi = pl.program_id(1)
    rows = GROUP * Q_BLOCK
    lane = lax.broadcasted_iota(jnp.int32, (Q_BLOCK, LANES), 1)
    r_idx = lax.broadcasted_iota(jnp.int32, (rows, 2 * LANES), 0) & (Q_BLOCK - 1)
    c_idx = lax.broadcasted_iota(jnp.int32, (rows, 2 * LANES), 1)
    first_col = jnp.where(qi > 0, 0, WINDOW)
    mask = (c_idx >= r_idx) & (c_idx <= r_idx + WINDOW) & (c_idx >= first_col)
    prev = jnp.maximum(qi - 1, 0)
    for j in range(N_PAIR):
        kt = jnp.concatenate([kvb_ref[prev, LANES * j:LANES * (j + 1), :],
                              kvb_ref[qi, LANES * j:LANES * (j + 1), :]], axis=1)
        vt = jnp.concatenate([kvb_ref[prev, KV_DIM + LANES * j:KV_DIM + LANES * (j + 1), :],
                              kvb_ref[qi, KV_DIM + LANES * j:KV_DIM + LANES * (j + 1), :]], axis=1)
        for p in range(2):
            in_half = (lane // HEAD_DIM) == p
            qs = jnp.concatenate(
                [jnp.where(in_half, q_ref[:, LANES * (GROUP * j + g):LANES * (GROUP * j + g + 1)], 0)
                 for g in range(GROUP)], axis=0)
            sink = jnp.concatenate(
                [jnp.full((Q_BLOCK, 1), sink_ref[2 * (GROUP * j + g) + p], F32) for g in range(GROUP)],
                axis=0)
            s = jnp.where(mask, jnp.dot(qs, kt, preferred_element_type=F32), -jnp.inf)
            m = jnp.maximum(jnp.max(s, axis=-1, keepdims=True), sink)
            e = jnp.exp(s - m)
            denom = jnp.sum(e, axis=-1, keepdims=True) + jnp.exp(sink - m)
            o = lax.dot_general(e.astype(BF16), vt, (((1,), (1,)), ((), ())),
                                preferred_element_type=F32) / denom
            if p == 0:
                half_ref[...] = o
            else:
                lane_r = lax.broadcasted_iota(jnp.int32, (rows, LANES), 1)
                o_pair = jnp.where((lane_r // HEAD_DIM) == 0, half_ref[...], o).astype(BF16)
                for g in range(GROUP):
                    o_ref[:, LANES * (GROUP * j + g):LANES * (GROUP * j + g + 1)] = \
                        o_pair[Q_BLOCK * g:Q_BLOCK * (g + 1)]


def _swa_prompt(q, kvb, sink_perm):
    b, t, _ = q.shape
    nq = t // Q_BLOCK
    nck = kvb.shape[1]
    return pl.pallas_call(
        _swa_prompt_kernel,
        grid_spec=pltpu.PrefetchScalarGridSpec(
            num_scalar_prefetch=1,
            grid=(b, nq),
            in_specs=[
                pl.BlockSpec((None, Q_BLOCK, Q_DIM), lambda bi, qi, s: (bi, qi, 0)),
                pl.BlockSpec((None, nck, 2 * KV_DIM, LANES), lambda bi, qi, s: (bi, 0, 0, 0)),
            ],
            out_specs=pl.BlockSpec((None, Q_BLOCK, Q_DIM), lambda bi, qi, s: (bi, qi, 0)),
            scratch_shapes=[pltpu.VMEM((GROUP * Q_BLOCK, LANES), F32)],
        ),
        out_shape=jax.ShapeDtypeStruct((b, t, Q_DIM), BF16),
        compiler_params=_cparams(("arbitrary", "arbitrary")),
        name="swa_prompt",
    )(sink_perm, q, kvb)


def _swa_sample_kernel(qe_ref, sink_ref, kb_ref, vb_ref, kn_ref, vn_ref, o_ref, ko_ref, vo_ref,
                       *, n_seq, dt, past_len):
    r = qe_ref.shape[1]
    wb = kb_ref.shape[2]
    t_row = lax.broadcasted_iota(jnp.int32, (r, wb), 0) % dt
    col = lax.broadcasted_iota(jnp.int32, (r, wb), 1)
    dist_old = wb + t_row - col
    mask_old = (dist_old <= WINDOW) & (col >= wb - past_len)
    mask_new = (col >= wb - dt) & (col - (wb - dt) <= t_row)
    lane_k = lax.broadcasted_iota(jnp.int32, (KV_DIM, wb), 1)
    sink = sink_ref[...][:, 0:1]
    for i in range(n_seq):
        qe = qe_ref[i]
        kb, vb, kn, vn = kb_ref[i], vb_ref[i], kn_ref[i], vn_ref[i]
        s_old = jnp.where(mask_old, jnp.dot(qe, kb.astype(BF16), preferred_element_type=F32), -jnp.inf)
        s_new = jnp.where(mask_new, jnp.dot(qe, kn.astype(BF16), preferred_element_type=F32), -jnp.inf)
        m = jnp.maximum(jnp.maximum(jnp.max(s_old, axis=-1, keepdims=True),
                                    jnp.max(s_new, axis=-1, keepdims=True)), sink)
        e_old = jnp.exp(s_old - m)
        e_new = jnp.exp(s_new - m)
        denom = (jnp.sum(e_old, axis=-1, keepdims=True) + jnp.sum(e_new, axis=-1, keepdims=True)
                 + jnp.exp(sink - m))
        nt = (((1,), (1,)), ((), ()))
        o = (lax.dot_general(e_old.astype(BF16), vb.astype(BF16), nt, preferred_element_type=F32)
             + lax.dot_general(e_new.astype(BF16), vn.astype(BF16), nt, preferred_element_type=F32))
        o_ref[i] = o / denom
        keep = lane_k < wb - dt
        ko_ref[i] = jnp.where(keep, pltpu.roll(kb, wb - dt, 1), kn)
        vo_ref[i] = jnp.where(keep, pltpu.roll(vb, wb - dt, 1), vn)


def _swa_sample(qe, sink_rows, buf_kt, buf_vt, layer, knew, vnew, dt, past_len):
    db, r, _ = qe.shape
    wb = buf_kt.shape[-1]
    n_seq = 4 if db % 4 == 0 else 1
    seq_spec = lambda shape: pl.BlockSpec((n_seq,) + shape, lambda b: (b, 0, 0))
    buf_spec = pl.BlockSpec((None, n_seq, KV_DIM, wb), lambda b: (layer, b, 0, 0))
    return pl.pallas_call(
        functools.partial(_swa_sample_kernel, n_seq=n_seq, dt=dt, past_len=past_len),
        grid=(db // n_seq,),
        in_specs=[seq_spec((r, KV_DIM)), pl.BlockSpec((r, LANES), lambda b: (0, 0)),
                  buf_spec, buf_spec, seq_spec((KV_DIM, wb)), seq_spec((KV_DIM, wb))],
        out_specs=[seq_spec((r, KV_DIM)), seq_spec((KV_DIM, wb)), seq_spec((KV_DIM, wb))],
        out_shape=[jax.ShapeDtypeStruct((db, r, KV_DIM), F32),
                   jax.ShapeDtypeStruct((db, KV_DIM, wb), F32),
                   jax.ShapeDtypeStruct((db, KV_DIM, wb), F32)],
        compiler_params=_cparams(("arbitrary",)),
        name="swa_sample",
    )(qe, sink_rows, buf_kt, buf_vt, knew, vnew)


_GELU_C0 = 0.7978845608028654
_GELU_C1 = 0.7978845608028654 * 0.044715


def _gelu_tanh(x):
    t = jnp.tanh(x * (_GELU_C0 + _GELU_C1 * (x * x)))
    return x * (0.5 + 0.5 * t)


def _ffn_kernel(o_ref, x_ref, gm_ref, sh_ref, sc_ref, gf_ref, gpm_ref, gpre_ref, gpost_ref, wo_ref,
                w_ref, cw_ref, cb_ref, wd_ref, prev_ref,
                out_ref, tail_ref,
                x1_ref, h_ref, act_ref, buf0_ref, buf1_ref, carry_ref,
                *, halo, stride, multi_tile, rb, tf):
    ti = pl.program_id(1)
    c = pl.program_id(2)
    nc = act_ref.shape[0]
    tm = x_ref.shape[0]
    bufs = (buf0_ref, buf1_ref)

    def project(cc, buf):
        if multi_tile:
            first = jnp.full((halo, 2 * tf), ti, jnp.int32) == 0
            buf[0:halo, :] = jnp.where(first, prev_ref[...], carry_ref[cc])
        else:
            buf[0:halo, :] = prev_ref[...]
        buf[halo:halo + tm, :] = jnp.dot(h_ref[...], w_ref[...], preferred_element_type=F32)
        last = buf[tm:tm + halo, :]
        if multi_tile:
            carry_ref[cc] = last
            tail_ref[cc] = last
        else:
            tail_ref[...] = last

    def activate(cc, buf):
        cw, cb = cw_ref[...], cb_ref[...]
        for r0 in range(0, tm, rb):
            u0 = buf[halo + r0:halo + r0 + rb, :]
            u1 = buf[halo - stride + r0:halo - stride + r0 + rb, :]
            u2 = buf[halo - 2 * stride + r0:halo - 2 * stride + r0 + rb, :]
            conv = cb + cw[0:1] * u2 + cw[1:2] * u1 + cw[2:3] * u0
            act_ref[cc, r0:r0 + rb, :] = (_gelu_tanh(conv[:, :tf]) * conv[:, tf:]).astype(BF16)

    halves = [(0, tm)] if tm % 16 else [(0, tm // 2), (tm // 2, tm)]

    def mod_rows(ref, lo, hi):
        m = ref[...]
        return m if m.shape[0] == 1 else _rows(m, tm)[lo:hi]

    @pl.when(c == 0)
    def _():
        ys = [jnp.dot(o_ref[lo:hi, :], wo_ref[...], preferred_element_type=F32) for lo, hi in halves]
        for (lo, hi), y in zip(halves, ys):
            x1 = x_ref[lo:hi, :] + mod_rows(gm_ref, lo, hi) * _rms(y, gpm_ref[...])
            x1_ref[lo:hi, :] = x1
            h_ref[lo:hi, :] = (_rms(x1, gpre_ref[...]) * (1.0 + mod_rows(sc_ref, lo, hi))
                               + mod_rows(sh_ref, lo, hi)).astype(BF16)
        project(c, bufs[0])

    for parity in range(2):
        @pl.when((c > 0) & (c < nc) & (lax.rem(c, 2) == parity))
        def _():
            activate(c - 1, bufs[1 - parity])
            project(c, bufs[parity])

    @pl.when(c == nc)
    def _():
        activate(c - 1, bufs[(nc - 1) % 2])
        kd = (nc - 1) * tf
        for lo, hi in halves:
            y = jnp.dot(act_ref[nc - 1, lo:hi, :], wd_ref[kd:, :], preferred_element_type=F32)
            if nc > 1:
                early = jnp.concatenate([act_ref[k, lo:hi, :] for k in range(nc - 1)], axis=1)
                y = y + jnp.dot(early, wd_ref[:kd, :], preferred_element_type=F32)
            out_ref[lo:hi, :] = x1_ref[lo:hi, :] + mod_rows(gf_ref, lo, hi) * _rms(y, gpost_ref[...])


def _interleave(a, tf):
    lead = a.shape[:-1]
    nc = a.shape[-1] // (2 * tf)
    nd = len(lead)
    a = a.reshape(lead + (2, nc, tf)).transpose(tuple(range(nd)) + (nd + 1, nd, nd + 2))
    return a.reshape(lead + (2 * nc * tf,))


def _deinterleave(a, tf):
    lead = a.shape[:-1]
    nc = a.shape[-1] // (2 * tf)
    nd = len(lead)
    a = a.reshape(lead + (nc, 2, tf)).transpose(tuple(range(nd)) + (nd + 1, nd, nd + 2))
    return a.reshape(lead + (2 * nc * tf,))


def _ffn(o, x, gate_m, shift, scale, gate_f, gpost_m, gpre, gpost, wo, w_up, cw, cb, wd, prev,
         *, tm, tf, halo, stride, mod_map):
    nb, r, d = x.shape
    dff = wd.shape[0]
    nc = dff // tf
    nt = r // tm
    mr = shift.shape[1]
    multi_tile = nt > 1
    rb = 64 if tm % 64 == 0 else tm
    wide = 2 * tf
    last = nc - 1
    once = pl.Buffered(1)
    mod_spec = pl.BlockSpec((None, mr, d), lambda b, t, c: (mod_map(b), 0, 0))
    vec_spec = pl.BlockSpec((1, d), lambda b, t, c: (0, 0))
    proj_spec = lambda rows: pl.BlockSpec((rows, wide), lambda b, t, c: (0, jnp.minimum(c, last)))
    act_spec = lambda rows: pl.BlockSpec((rows, wide), lambda b, t, c: (0, jnp.maximum(c - 1, 0)))
    prev_spec = pl.BlockSpec((None, halo, wide), lambda b, t, c: (b, 0, jnp.minimum(c, last)))
    if multi_tile:
        tail_spec = pl.BlockSpec((None, nc, halo, wide), lambda b, t, c: (b, 0, 0, 0))
        tail_shape = jax.ShapeDtypeStruct((nb, nc, halo, wide), F32)
        row_mode = {}
    else:
        tail_spec = prev_spec
        tail_shape = jax.ShapeDtypeStruct((nb, halo, 2 * dff), F32)
        row_mode = dict(pipeline_mode=once) if nb == 1 else {}
    out, tail = pl.pallas_call(
        functools.partial(_ffn_kernel, halo=halo, stride=stride, multi_tile=multi_tile, rb=rb, tf=tf),
        grid=(nb, nt, nc + 1),
        in_specs=[
            pl.BlockSpec((None, tm, Q_DIM), lambda b, t, c: (b, t, 0), **row_mode),
            pl.BlockSpec((None, tm, d), lambda b, t, c: (b, t, 0), **row_mode),
            mod_spec, mod_spec, mod_spec, mod_spec, vec_spec, vec_spec, vec_spec,
            pl.BlockSpec(wo.shape, lambda b, t, c: (0, 0), pipeline_mode=once),
            proj_spec(d), act_spec(3), act_spec(1),
            pl.BlockSpec(wd.shape, lambda b, t, c: (0, 0), pipeline_mode=once),
            prev_spec,
        ],
        out_specs=[pl.BlockSpec((None, tm, d), lambda b, t, c: (b, t, 0)), tail_spec],
        out_shape=[jax.ShapeDtypeStruct((nb, r, d), F32), tail_shape],
        scratch_shapes=[
            pltpu.VMEM((tm, d), F32), pltpu.VMEM((tm, d), BF16), pltpu.VMEM((nc, tm, tf), BF16),
            pltpu.VMEM((halo + tm, wide), F32), pltpu.VMEM((halo + tm, wide), F32),
            pltpu.VMEM((nc, halo, wide) if multi_tile else (1, 8, LANES), F32),
        ],
        compiler_params=_cparams(("arbitrary", "arbitrary", "arbitrary")),
        name="mixer_out_conv_ffn",
    )(o, x, gate_m, shift, scale, gate_f, gpost_m, gpre, gpost, wo, w_up, cw, cb, wd, prev)
    if multi_tile:
        tail = tail.transpose(0, 2, 1, 3).reshape(nb, halo, 2 * dff)
    return out, tail


def _rope_tables(pos):
    inv_freq = jnp.power(jnp.float32(ROPE_THETA), -jnp.arange(ROT_HALF, dtype=F32) * (2.0 / ROT_DIM))
    ang = pos.astype(F32)[:, None] * inv_freq[None, :]
    cos, sin = jnp.cos(ang), jnp.sin(ang)
    n = pos.shape[0]
    ones = jnp.ones((n, HEAD_DIM - ROT_DIM), F32)
    zeros_r = jnp.zeros((n, HEAD_DIM - ROT_DIM), F32)
    zeros_h = jnp.zeros((n, ROT_HALF), F32)
    cq = jnp.concatenate([cos, cos, ones], axis=1)
    s1 = jnp.concatenate([zeros_h, sin, zeros_r], axis=1)
    s2 = jnp.concatenate([-sin, zeros_h, zeros_r], axis=1)
    rep = LANES // HEAD_DIM
    return (jnp.tile(cq, (1, rep)), jnp.tile(s1, (1, rep)), jnp.tile(s2, (1, rep)), cos.T, sin.T)


def _kv_onehot():
    oh = np.zeros((N_QBLK, 2, N_KV_HEADS), np.float32)
    for i in range(N_QBLK):
        for p in range(2):
            oh[i, p, 2 * (i // GROUP) + p] = 1.0
    return oh


def _expand_q(q_tm):
    dt, db, _ = q_tm.shape
    q5 = q_tm.reshape(dt, db, N_QBLK, 2, HEAD_DIM).transpose(1, 2, 3, 0, 4)
    oh = jnp.asarray(_kv_onehot(), q_tm.dtype)
    q6 = q5[:, :, :, :, None, :] * oh[None, :, :, None, :, None]
    return q6.reshape(db, N_QBLK * 2 * dt, KV_DIM)


def _select_o(o_full, dt):
    db = o_full.shape[0]
    o6 = o_full.reshape(db, N_QBLK, 2, dt, N_KV_HEADS, HEAD_DIM)
    o5 = jnp.sum(o6 * jnp.asarray(_kv_onehot())[None, :, :, None, :, None], axis=4)
    return o5.transpose(3, 0, 1, 2, 4).reshape(dt, db, Q_DIM).astype(BF16)


def _rows_of_heads(v, dt):
    vp = v[jnp.asarray(HEAD_PERM)]
    return jnp.broadcast_to(vp[:, None, None], (N_HEADS, dt, LANES)).reshape(N_HEADS * dt, LANES).astype(F32)


def _from_t(kt, lead):
    n = kt.shape[-1]
    x = kt.reshape(lead + (N_KV_HEADS, HEAD_DIM, n))
    nd = len(lead)
    return x.transpose(tuple(range(nd)) + (nd + 2, nd, nd + 1))


def kernel(x_prompt, x_sample, cache_sb_k, cache_sb_v, state_swa_k, state_swa_v, state_conv, page_table,
           c_prompt, c_sample, mod_w, mod_b, norm_mix_pre, norm_mix_post, norm_ffn_pre, norm_ffn_post,
           sb_w_qkv, sb_w_o, sb_bias, swa_w_qkv, swa_w_o, swa_sinks, ffn_w_up, ffn_conv_w, ffn_conv_b,
           ffn_w_down):
    b, t, d = x_prompt.shape
    db, dt, _ = x_sample.shape
    depth = mod_w.shape[0]
    n_pages = page_table.shape[1]
    past_len = n_pages * PAGE_SIZE
    wb = state_swa_k.shape[2]
    dff = ffn_w_down.shape[1]
    n_phys = cache_sb_k.shape[1]

    tm_p = min(512, t)
    tm_f = min(512, t)
    tf = 256

    mod = _modulation(jnp.concatenate([c_prompt, c_sample], axis=0), mod_w, mod_b)
    mod_p = mod[:, :, :b].reshape(depth, N_MOD, b, 1, d)
    mod_s = mod[:, :, b:].reshape(depth, N_MOD, 1, db, d)

    cache_kt = cache_sb_k.transpose(0, 1, 3, 4, 2).reshape(-1, n_phys, KV_DIM, PAGE_SIZE)
    cache_vt = cache_sb_v.transpose(0, 1, 3, 4, 2).reshape(-1, n_phys, KV_DIM, PAGE_SIZE)
    swa_kt = state_swa_k.transpose(0, 1, 3, 4, 2).reshape(-1, db, KV_DIM, wb)
    swa_vt = state_swa_v.transpose(0, 1, 3, 4, 2).reshape(-1, db, KV_DIM, wb)

    qcols = jnp.asarray(np.concatenate([np.arange(HEAD_DIM) + HEAD_DIM * h for h in HEAD_PERM]))
    perm = jnp.asarray(HEAD_PERM)

    rope_p = _rope_tables(jnp.arange(t))
    rope_p = tuple(a[None] for a in rope_p) + (False,)
    cq, s1, s2, ck, sk = _rope_tables(past_len + jnp.arange(dt))
    rope_s = (cq[:, None, :], s1[:, None, :], s2[:, None, :],
              jnp.broadcast_to(ck.T[:, :, None], (dt, ROT_HALF, db)),
              jnp.broadcast_to(sk.T[:, :, None], (dt, ROT_HALF, db)), True)

    xp = x_prompt
    xs = x_sample.transpose(1, 0, 2)
    pmap = lambda bi: bi
    smap = lambda bi: 0

    outs = {k: [] for k in ("sbkp", "sbvp", "sbks", "sbvs", "swkp", "swvp", "swks", "swvs", "cp", "cs")}
    for i in range(depth):
        j = i // 2
        is_sb = i % 2 == 0
        w_qkv = (sb_w_qkv if is_sb else swa_w_qkv)[j]
        w_o = (sb_w_o if is_sb else swa_w_o)[j]
        wq = w_qkv[:, :Q_DIM][:, qcols].astype(BF16)
        wkvt = w_qkv[:, Q_DIM:].T.astype(BF16)
        wo = w_o[qcols, :].astype(BF16)
        g_pre = norm_mix_pre[i][None]
        g_post = norm_mix_post[i][None]
        mp = [mod_p[i, k] for k in range(N_MOD)]
        ms = [mod_s[i, k] for k in range(N_MOD)]

        qp, ktp, vtp, kvbp = _qkv(xp, mp[0], mp[1], g_pre, wq, wkvt, None if is_sb else rope_p,
                                  tm=tm_p, mod_map=pmap)
        qs, kts, vts, _ = _qkv(xs, ms[0], ms[1], g_pre, wq, wkvt, None if is_sb else rope_s,
                               tm=db, mod_map=smap)
        qe = _expand_q(qs)
        kn = kts.transpose(2, 1, 0)
        vn = vts.transpose(2, 1, 0)
        if is_sb:
            op = _sb_prompt(qp, kvbp, sb_bias[j][perm])
            pad = ((0, 0), (0, 0), (0, LANES - dt))
            o_full = _sb_sample(page_table, qe, _rows_of_heads(sb_bias[j], dt),
                                jnp.pad(kn, pad).astype(BF16), jnp.pad(vn, pad).astype(BF16),
                                cache_kt, cache_vt, j, dt)
            outs["sbkp"].append(_from_t(ktp, (b,)))
            outs["sbvp"].append(_from_t(vtp, (b,)))
            outs["sbks"].append(_from_t(kn, (db,)))
            outs["sbvs"].append(_from_t(vn, (db,)))
        else:
            op = _swa_prompt(qp, kvbp, swa_sinks[j][perm])
            pad = ((0, 0), (0, 0), (wb - dt, 0))
            o_full, ko, vo = _swa_sample(qe, _rows_of_heads(swa_sinks[j], dt), swa_kt, swa_vt, j,
                                         jnp.pad(kn, pad), jnp.pad(vn, pad), dt, past_len)
            keep = min(WINDOW, t)
            outs["swkp"].append(_from_t(ktp[:, :, t - keep:], (b,)))
            outs["swvp"].append(_from_t(vtp[:, :, t - keep:], (b,)))
            outs["swks"].append(_from_t(ko, (db,)))
            outs["swvs"].append(_from_t(vo, (db,)))
        os_ = _select_o(o_full, dt)

        w_up = _interleave(ffn_w_up[i], tf).astype(BF16)
        wd = ffn_w_down[i].astype(BF16)
        cw = _interleave(ffn_conv_w[i], tf)
        cb = _interleave(ffn_conv_b[i][None], tf)
        gf_pre = norm_ffn_pre[i][None]
        gf_post = norm_ffn_post[i][None]

        halo_p = 8
        zero_prev = jnp.zeros((b, halo_p, 2 * dff), F32)
        xp, tail_p = _ffn(op, xp, mp[2], mp[3], mp[4], mp[5], g_post, gf_pre, gf_post, wo,
                          w_up, cw, cb, wd, zero_prev,
                          tm=tm_f, tf=tf, halo=halo_p, stride=1, mod_map=pmap)
        st = state_conv[i].transpose(1, 0, 2).reshape(1, 2 * db, 2 * dff)
        xs1, tail_s = _ffn(os_.reshape(1, dt * db, Q_DIM), xs.reshape(1, dt * db, d),
                           ms[2], ms[3], ms[4], ms[5], g_post, gf_pre, gf_post, wo,
                           w_up, cw, cb, wd, _interleave(st, tf),
                           tm=dt * db, tf=tf, halo=2 * db, stride=db, mod_map=smap)
        xs = xs1.reshape(dt, db, d)
        outs["cp"].append(_deinterleave(tail_p[:, halo_p - 2:], tf))
        outs["cs"].append(_deinterleave(tail_s, tf).reshape(2, db, 2 * dff).transpose(1, 0, 2))

    st_ = lambda k: jnp.stack(outs[k])
    return (xp, xs.transpose(1, 0, 2),
            st_("sbkp"), st_("sbvp"), st_("sbks"), st_("sbvs"),
            st_("swkp"), st_("swvp"), st_("swks"), st_("swvs"),
            st_("cp"), st_("cs"))
```

```python
import functools

import numpy as np
import jax
import jax.numpy as jnp
from jax import lax
from jax.experimental import pallas as pl
from jax.experimental.pallas import tpu as pltpu

F32 = jnp.float32
BF16 = jnp.bfloat16

HEAD_DIM = 64
N_HEADS = 16
N_KV_HEADS = 4
GROUP = N_HEADS // N_KV_HEADS
Q_DIM = N_HEADS * HEAD_DIM
KV_DIM = N_KV_HEADS * HEAD_DIM
PAGE_SIZE = 128
WINDOW = 128
Q_BLOCK = 128
ROT_DIM = HEAD_DIM // 4
ROT_HALF = ROT_DIM // 2
ROPE_THETA = 500000.0
RMS_EPS = 1e-6
N_MOD = 6
LANES = 128
N_QBLK = Q_DIM // LANES
N_PAIR = KV_DIM // LANES
VMEM_LIMIT = 56 * 1024 * 1024

HEAD_PERM = [4 * (2 * (i // 4) + p) + (i % 4) for i in range(N_QBLK) for p in range(2)]


def _cparams(sem):
    return pltpu.CompilerParams(dimension_semantics=sem, vmem_limit_bytes=VMEM_LIMIT)


def _rms(x, g):
    return x * lax.rsqrt(jnp.mean(x * x, axis=-1, keepdims=True) + RMS_EPS) * g


def _rows(m, rows):
    mr = m.shape[0]
    if mr == 1 or mr == rows:
        return m
    return jnp.concatenate([m] * (rows // mr), axis=0)


def _norm_mod(x, g, shift, scale):
    rows = x.shape[0]
    return _rms(x, g) * (1.0 + _rows(scale, rows)) + _rows(shift, rows)


def _mod_kernel(c_ref, w_ref, b_ref, o_ref):
    c = c_ref[...]
    s = c / (1.0 + jnp.exp(-c))
    o_ref[...] = jnp.dot(s.astype(BF16), w_ref[...].astype(BF16), preferred_element_type=F32) + b_ref[...]


def _modulation(c_all, mod_w, mod_b):
    depth, d, _ = mod_w.shape
    n = c_all.shape[0]
    return pl.pallas_call(
        _mod_kernel,
        grid=(depth, N_MOD),
        in_specs=[
            pl.BlockSpec((n, d), lambda i, k: (0, 0)),
            pl.BlockSpec((None, d, d), lambda i, k: (i, 0, k)),
            pl.BlockSpec((None, None, 1, d), lambda i, k: (i, k, 0, 0)),
        ],
        out_specs=pl.BlockSpec((None, None, n, d), lambda i, k: (i, k, 0, 0)),
        out_shape=jax.ShapeDtypeStruct((depth, N_MOD, n, d), F32),
        compiler_params=_cparams(("arbitrary", "arbitrary")),
        name="modulation",
    )(c_all, mod_w, mod_b.reshape(depth, N_MOD, 1, d))


def _qkv_kernel(*refs, rope, kc):
    if rope:
        (x_ref, sh_ref, sc_ref, g_ref, wq_ref, wkv_ref, cq_ref, s1_ref, s2_ref, ck_ref, sk_ref,
         q_ref, kt_ref, vt_ref, kvb_ref) = refs
    else:
        x_ref, sh_ref, sc_ref, g_ref, wq_ref, wkv_ref, q_ref, kt_ref, vt_ref, kvb_ref = refs
    tm = x_ref.shape[0]
    h = _norm_mod(x_ref[...], g_ref[...], sh_ref[...], sc_ref[...]).astype(BF16)
    q = jnp.dot(h, wq_ref[...], preferred_element_type=F32)
    kvt = lax.dot_general(wkv_ref[...], h, (((1,), (1,)), ((), ())),
                          preferred_element_type=F32)
    scale = HEAD_DIM ** -0.5
    if rope:
        cq, s1, s2 = cq_ref[...], s1_ref[...], s2_ref[...]
        for i in range(N_QBLK):
            xr = q[:, LANES * i:LANES * (i + 1)]
            rot = xr * cq + pltpu.roll(xr, ROT_HALF, 1) * s1 + pltpu.roll(xr, LANES - ROT_HALF, 1) * s2
            q_ref[:, LANES * i:LANES * (i + 1)] = (rot * scale).astype(BF16)
        ck, sk = ck_ref[...], sk_ref[...]
        pieces = []
        for c in range(N_KV_HEADS):
            base = HEAD_DIM * c
            x1 = kvt[base:base + ROT_HALF]
            x2 = kvt[base + ROT_HALF:base + ROT_DIM]
            pieces += [x1 * ck - x2 * sk, x2 * ck + x1 * sk, kvt[base + ROT_DIM:base + HEAD_DIM]]
        kt = jnp.concatenate(pieces, axis=0)
    else:
        q_ref[...] = (q * scale).astype(BF16)
        kt = kvt[:KV_DIM]
    vt = kvt[KV_DIM:]
    kt_ref[...] = kt
    vt_ref[...] = vt
    kb = kt.astype(BF16)
    vb = vt.astype(BF16)
    for c in range(tm // kc):
        kvb_ref[c, :KV_DIM, :] = kb[:, kc * c:kc * (c + 1)]
        kvb_ref[c, KV_DIM:, :] = vb[:, kc * c:kc * (c + 1)]


def _qkv(x, shift, scale, g, wq, wkvt, rope_tabs, *, tm, mod_map):
    nb, r, d = x.shape
    kc = min(LANES, tm)
    nt = r // tm
    rope = rope_tabs is not None
    mr = shift.shape[1]
    in_specs = [
        pl.BlockSpec((None, tm, d), lambda b, t: (b, t, 0)),
        pl.BlockSpec((None, mr, d), lambda b, t: (mod_map(b), 0, 0)),
        pl.BlockSpec((None, mr, d), lambda b, t: (mod_map(b), 0, 0)),
        pl.BlockSpec((1, d), lambda b, t: (0, 0)),
        pl.BlockSpec(wq.shape, lambda b, t: (0, 0)),
        pl.BlockSpec(wkvt.shape, lambda b, t: (0, 0)),
    ]
    args = [x, shift, scale, g, wq, wkvt]
    if rope:
        cq, s1, s2, ck, sk, per_b = rope_tabs
        if per_b:
            qspec = pl.BlockSpec((None, 1, LANES), lambda b, t: (b, 0, 0))
            kspec = pl.BlockSpec((None, ROT_HALF, tm), lambda b, t: (b, 0, 0))
        else:
            qspec = pl.BlockSpec((None, tm, LANES), lambda b, t: (0, t, 0))
            kspec = pl.BlockSpec((None, ROT_HALF, tm), lambda b, t: (0, 0, t))
        in_specs += [qspec, qspec, qspec, kspec, kspec]
        args += [cq, s1, s2, ck, sk]
    out_shape = [
        jax.ShapeDtypeStruct((nb, r, Q_DIM), BF16),
        jax.ShapeDtypeStruct((nb, KV_DIM, r), F32),
        jax.ShapeDtypeStruct((nb, KV_DIM, r), F32),
        jax.ShapeDtypeStruct((nb, r // kc, 2 * KV_DIM, kc), BF16),
    ]
    out_specs = [
        pl.BlockSpec((None, tm, Q_DIM), lambda b, t: (b, t, 0)),
        pl.BlockSpec((None, KV_DIM, tm), lambda b, t: (b, 0, t)),
        pl.BlockSpec((None, KV_DIM, tm), lambda b, t: (b, 0, t)),
        pl.BlockSpec((None, tm // kc, 2 * KV_DIM, kc), lambda b, t: (b, t, 0, 0)),
    ]
    return pl.pallas_call(
        functools.partial(_qkv_kernel, rope=rope, kc=kc),
        grid=(nb, nt),
        in_specs=in_specs,
        out_specs=out_specs,
        out_shape=out_shape,
        compiler_params=_cparams(("arbitrary", "arbitrary")),
        name="qkv_rope" if rope else "qkv",
    )(*args)


LOG2E = 1.4426950408889634


def _sb_split(z, mask):
    n = z.shape[1] // LANES
    sp = jnp.maximum(z, 0.0) + jnp.log(1.0 + jnp.exp2(jnp.abs(z) * (-LOG2E)))
    if mask is not None:
        sp = jnp.where(mask, sp, 0.0)
    hi_f = lax.bitcast_convert_type(
        lax.bitcast_convert_type(sp, jnp.uint32) & jnp.uint32(0xFFFF0000), F32)
    hi = hi_f.astype(BF16)
    lo = (sp - hi_f).astype(BF16)
    return jnp.concatenate(
        [jnp.concatenate([hi[:, LANES * k:LANES * (k + 1)], lo[:, LANES * k:LANES * (k + 1)]], axis=1)
         for k in range(n)], axis=0)


def _sb_finish(z, t, mask, carry):
    r = z.shape[0]
    n = z.shape[1] // LANES
    tails = [None] * n
    for k in reversed(range(n)):
        blk = t[r * k:r * (k + 1)]
        tails[k] = blk[:, :LANES] + carry
        carry = carry + blk[:, LANES:]
    tail = tails[0] if n == 1 else jnp.concatenate(tails, axis=1)
    a = jnp.exp(z + tail)
    if mask is not None:
        a = jnp.where(mask, a, 0.0)
    return a.astype(BF16), carry


def _cumsum_total_matrix():
    j = np.arange(LANES)[:, None]
    c = np.arange(LANES)[None, :]
    half = np.concatenate([-(j >= c).astype(np.float32), -np.ones((LANES, LANES), np.float32)], axis=1)
    return jnp.asarray(np.concatenate([half, half], axis=0), dtype=BF16)


def _sb_prompt_kernel(bias_ref, q_ref, kvb_ref, w_ref, o_ref, *scratch):
    qi = pl.program_id(1)
    rows = GROUP * Q_BLOCK
    lane = lax.broadcasted_iota(jnp.int32, (Q_BLOCK, LANES), 1)
    top = qi // 2
    qs_refs, c_refs, acc_refs = scratch[0:4], scratch[4:8], scratch[8:12]
    chains = [(j, p) for j in range(N_PAIR) for p in range(2)]
    nt_dims = (((1,), (1,)), ((), ()))
    for n, (j, p) in enumerate(chains):
        in_half = (lane // HEAD_DIM) == p
        for g in range(GROUP):
            blk = q_ref[:, LANES * (GROUP * j + g):LANES * (GROUP * j + g + 1)]
            qs_refs[n][Q_BLOCK * g:Q_BLOCK * (g + 1), :] = jnp.where(in_half, blk, 0)
        c_refs[n][...] = jnp.zeros_like(c_refs[n])
        acc_refs[n][...] = jnp.zeros_like(acc_refs[n])

    def chunk(ci, masked):
        kts = [jnp.concatenate([kvb_ref[2 * ci, LANES * j:LANES * (j + 1), :],
                                kvb_ref[2 * ci + 1, LANES * j:LANES * (j + 1), :]], axis=1)
               for j in range(N_PAIR)]
        vts = [jnp.concatenate([kvb_ref[2 * ci, KV_DIM + LANES * j:KV_DIM + LANES * (j + 1), :],
                                kvb_ref[2 * ci + 1, KV_DIM + LANES * j:KV_DIM + LANES * (j + 1), :]], axis=1)
               for j in range(N_PAIR)]
        mask = None
        if masked:
            r_idx = lax.broadcasted_iota(jnp.int32, (rows, 2 * LANES), 0) & (Q_BLOCK - 1)
            c_idx = lax.broadcasted_iota(jnp.int32, (rows, 2 * LANES), 1)
            mask = (ci * 2 * LANES + c_idx) < (qi * Q_BLOCK + r_idx)
        zs = [jnp.dot(qs_refs[n][...], kts[j], preferred_element_type=F32) for n, (j, p) in enumerate(chains)]
        zs = [jnp.concatenate(
            [zs[n][Q_BLOCK * g:Q_BLOCK * (g + 1)] + bias_ref[2 * (GROUP * j + g) + p]
             for g in range(GROUP)], axis=0) for n, (j, p) in enumerate(chains)]
        hls = [_sb_split(z, mask) for z in zs]
        ts = [jnp.dot(hl, w_ref[...], preferred_element_type=F32) for hl in hls]
        aw = [_sb_finish(zs[n], ts[n], mask, c_refs[n][...]) for n in range(len(chains))]
        for n in range(len(chains)):
            c_refs[n][...] = aw[n][1]
        pv = [lax.dot_general(aw[n][0], vts[j], nt_dims, preferred_element_type=F32)
              for n, (j, p) in enumerate(chains)]
        for n in range(len(chains)):
            acc_refs[n][...] += pv[n]

    chunk(top, True)

    def body(n, _):
        chunk(top - 1 - n, False)
        return 0

    lax.fori_loop(0, top, body, 0)
    lane_r = lax.broadcasted_iota(jnp.int32, (rows, LANES), 1)
    for j in range(N_PAIR):
        o_pair = jnp.where((lane_r // HEAD_DIM) == 0, acc_refs[2 * j][...],
                           acc_refs[2 * j + 1][...]).astype(BF16)
        for g in range(GROUP):
            o_ref[:, LANES * (GROUP * j + g):LANES * (GROUP * j + g + 1)] = \
                o_pair[Q_BLOCK * g:Q_BLOCK * (g + 1)]


def _sb_prompt(q, kvb, bias_perm):
    b, t, _ = q.shape
    nq = t // Q_BLOCK
    nck = kvb.shape[1]
    rows = GROUP * Q_BLOCK
    return pl.pallas_call(
        _sb_prompt_kernel,
        grid_spec=pltpu.PrefetchScalarGridSpec(
            num_scalar_prefetch=1,
            grid=(b, nq),
            in_specs=[
                pl.BlockSpec((None, Q_BLOCK, Q_DIM), lambda bi, qi, s: (bi, qi, 0)),
                pl.BlockSpec((None, nck, 2 * KV_DIM, LANES), lambda bi, qi, s: (bi, 0, 0, 0)),
                pl.BlockSpec((2 * LANES, 2 * LANES), lambda bi, qi, s: (0, 0)),
            ],
            out_specs=pl.BlockSpec((None, Q_BLOCK, Q_DIM), lambda bi, qi, s: (bi, qi, 0)),
            scratch_shapes=([pltpu.VMEM((rows, LANES), BF16)] * N_KV_HEADS
                            + [pltpu.VMEM((rows, LANES), F32)] * (2 * N_KV_HEADS)),
        ),
        out_shape=jax.ShapeDtypeStruct((b, t, Q_DIM), BF16),
        compiler_params=_cparams(("arbitrary", "arbitrary")),
        name="sb_prompt",
    )(bias_perm, q, kvb, _cumsum_total_matrix())


def _sb_sample_kernel(*refs, n_pp, dt):
    qe_ref, bias_ref, w_ref, kn_ref, vn_ref = refs[1:6]
    k_refs = refs[6:6 + n_pp]
    v_refs = refs[6 + n_pp:6 + 2 * n_pp]
    o_ref, c_ref, acc_ref = refs[6 + 2 * n_pp:]
    s = pl.program_id(1)
    qe = qe_ref[...]
    bias = bias_ref[...]
    r = qe.shape[0]
    nt_dims = (((1,), (1,)), ((), ()))

    @pl.when(s == 0)
    def _():
        t_row = lax.broadcasted_iota(jnp.int32, (r, LANES), 0) % dt
        t_key = lax.broadcasted_iota(jnp.int32, (r, LANES), 1)
        z = jnp.dot(qe, kn_ref[...], preferred_element_type=F32) + bias
        mask = t_key < t_row
        t = jnp.dot(_sb_split(z, mask), w_ref[...], preferred_element_type=F32)
        a, c_new = _sb_finish(z, t, mask, jnp.zeros((r, LANES), F32))
        c_ref[...] = c_new
        acc_ref[...] = lax.dot_general(a, vn_ref[...], nt_dims, preferred_element_type=F32)

    z = jnp.concatenate(
        [jnp.dot(qe, k_refs[k][...].astype(BF16), preferred_element_type=F32) + bias for k in range(n_pp)],
        axis=1)
    hl = _sb_split(z, None)
    hr = hl.shape[0] // 2 if n_pp % 2 == 0 else hl.shape[0]
    t = jnp.concatenate([jnp.dot(hl[i:i + hr], w_ref[...], preferred_element_type=F32)
                         for i in range(0, hl.shape[0], hr)], axis=0)
    a, c_new = _sb_finish(z, t, None, c_ref[...])
    c_ref[...] = c_new
    hk = (n_pp // 2 if n_pp % 2 == 0 else n_pp)
    pv = None
    for k0 in range(0, n_pp, hk):
        vt = jnp.concatenate([v_refs[k][...].astype(BF16) for k in range(k0, k0 + hk)], axis=1)
        part = lax.dot_general(a[:, PAGE_SIZE * k0:PAGE_SIZE * (k0 + hk)], vt, nt_dims,
                               preferred_element_type=F32)
        pv = part if pv is None else pv + part
    acc_ref[...] += pv

    @pl.when(s == pl.num_programs(1) - 1)
    def _():
        o_ref[...] = acc_ref[...]


def _sb_sample(page_table, qe, bias_rows, knew, vnew, cache_kt, cache_vt, layer, dt):
    db, r, _ = qe.shape
    n_pages = page_table.shape[1]
    n_pp = min(16, n_pages)
    while n_pages % n_pp:
        n_pp -= 1
    steps = n_pages // n_pp

    def page_spec(k):
        return pl.BlockSpec((None, None, KV_DIM, PAGE_SIZE),
                            lambda b, s, pt: (layer, pt[b, n_pages - n_pp * (s + 1) + k], 0, 0))

    in_specs = [
        pl.BlockSpec((None, r, KV_DIM), lambda b, s, pt: (b, 0, 0)),
        pl.BlockSpec((r, LANES), lambda b, s, pt: (0, 0)),
        pl.BlockSpec((2 * LANES, 2 * LANES), lambda b, s, pt: (0, 0)),
        pl.BlockSpec((None, KV_DIM, LANES), lambda b, s, pt: (b, 0, 0)),
        pl.BlockSpec((None, KV_DIM, LANES), lambda b, s, pt: (b, 0, 0)),
    ] + [page_spec(k) for k in range(n_pp)] * 2
    return pl.pallas_call(
        functools.partial(_sb_sample_kernel, n_pp=n_pp, dt=dt),
        grid_spec=pltpu.PrefetchScalarGridSpec(
            num_scalar_prefetch=1,
            grid=(db, steps),
            in_specs=in_specs,
            out_specs=pl.BlockSpec((None, r, KV_DIM), lambda b, s, pt: (b, 0, 0)),
            scratch_shapes=[pltpu.VMEM((r, LANES), F32), pltpu.VMEM((r, KV_DIM), F32)],
        ),
        out_shape=jax.ShapeDtypeStruct((db, r, KV_DIM), F32),
        compiler_params=_cparams(("arbitrary", "arbitrary")),
        name="sb_sample",
    )(page_table, qe, bias_rows, _cumsum_total_matrix(), knew, vnew,
      *([cache_kt] * n_pp), *([cache_vt] * n_pp))


def _swa_prompt_kernel(sink_ref, q_ref, kvb_ref, o_ref, half_ref):
    qi = pl.program_id(1)
    rows = GROUP * Q_BLOCK
    lane = lax.broadcasted_iota(jnp.int32, (Q_BLOCK, LANES), 1)
    r_idx = lax.broadcasted_iota(jnp.int32, (rows, 2 * LANES), 0) & (Q_BLOCK - 1)
    c_idx = lax.broadcasted_iota(jnp.int32, (rows, 2 * LANES), 1)
    first_col = jnp.where(qi > 0, 0, WINDOW)
    mask = (c_idx >= r_idx) & (c_idx <= r_idx + WINDOW) & (c_idx >= first_col)
    prev = jnp.maximum(qi - 1, 0)
    ones = jnp.ones((LANES, 2 * LANES), BF16)
    for j in range(N_PAIR):
        kt = jnp.concatenate([kvb_ref[prev, LANES * j:LANES * (j + 1), :],
                              kvb_ref[qi, LANES * j:LANES * (j + 1), :]], axis=1)
        vt = jnp.concatenate([kvb_ref[prev, KV_DIM + LANES * j:KV_DIM + LANES * (j + 1), :],
                              kvb_ref[qi, KV_DIM + LANES * j:KV_DIM + LANES * (j + 1), :]], axis=1)
        vt_ones = jnp.concatenate([vt, ones], axis=0)
        for p in range(2):
            in_half = (lane // HEAD_DIM) == p
            qs = jnp.concatenate(
                [jnp.where(in_half, q_ref[:, LANES * (GROUP * j + g):LANES * (GROUP * j + g + 1)], 0)
                 for g in range(GROUP)], axis=0)
            sink = jnp.concatenate(
                [jnp.full((Q_BLOCK, LANES), sink_ref[2 * (GROUP * j + g) + p], F32) for g in range(GROUP)],
                axis=0)
            s = jnp.where(mask, jnp.dot(qs, kt, preferred_element_type=F32), -jnp.inf)
            m = jnp.maximum(jnp.max(s, axis=-1, keepdims=True), sink)
            e = jnp.exp(s - jnp.concatenate([m, m], axis=1)).astype(BF16)
            pv = lax.dot_general(e, vt_ones, (((1,), (1,)), ((), ())), preferred_element_type=F32)
            o = pv[:, :LANES] / (pv[:, LANES:] + jnp.exp(sink - m))
            if p == 0:
                half_ref[...] = o
            else:
                lane_r = lax.broadcasted_iota(jnp.int32, (rows, LANES), 1)
                o_pair = jnp.where((lane_r // HEAD_DIM) == 0, half_ref[...], o).astype(BF16)
                for g in range(GROUP):
                    o_ref[:, LANES * (GROUP * j + g):LANES * (GROUP * j + g + 1)] = \
                        o_pair[Q_BLOCK * g:Q_BLOCK * (g + 1)]


def _swa_prompt(q, kvb, sink_perm):
    b, t, _ = q.shape
    nq = t // Q_BLOCK
    nck = kvb.shape[1]
    return pl.pallas_call(
        _swa_prompt_kernel,
        grid_spec=pltpu.PrefetchScalarGridSpec(
            num_scalar_prefetch=1,
            grid=(b, nq),
            in_specs=[
                pl.BlockSpec((None, Q_BLOCK, Q_DIM), lambda bi, qi, s: (bi, qi, 0)),
                pl.BlockSpec((None, nck, 2 * KV_DIM, LANES), lambda bi, qi, s: (bi, 0, 0, 0)),
            ],
            out_specs=pl.BlockSpec((None, Q_BLOCK, Q_DIM), lambda bi, qi, s: (bi, qi, 0)),
            scratch_shapes=[pltpu.VMEM((GROUP * Q_BLOCK, LANES), F32)],
        ),
        out_shape=jax.ShapeDtypeStruct((b, t, Q_DIM), BF16),
        compiler_params=_cparams(("arbitrary", "arbitrary")),
        name="swa_prompt",
    )(sink_perm, q, kvb)


def _swa_sample_kernel(qe_ref, sink_ref, kb_ref, vb_ref, kn_ref, vn_ref, o_ref, ko_ref, vo_ref,
                       *, n_seq, dt, past_len):
    r = qe_ref.shape[1]
    wb = kb_ref.shape[2]
    t_row = lax.broadcasted_iota(jnp.int32, (r, wb), 0) % dt
    col = lax.broadcasted_iota(jnp.int32, (r, wb), 1)
    dist_old = wb + t_row - col
    mask_old = (dist_old <= WINDOW) & (col >= wb - past_len)
    mask_new = (col >= wb - dt) & (col - (wb - dt) <= t_row)
    lane_k = lax.broadcasted_iota(jnp.int32, (KV_DIM, wb), 1)
    sink = sink_ref[...][:, 0:1]
    for i in range(n_seq):
        qe = qe_ref[i]
        kb, vb, kn, vn = kb_ref[i], vb_ref[i], kn_ref[i], vn_ref[i]
        s_old = jnp.where(mask_old, jnp.dot(qe, kb.astype(BF16), preferred_element_type=F32), -jnp.inf)
        s_new = jnp.where(mask_new, jnp.dot(qe, kn.astype(BF16), preferred_element_type=F32), -jnp.inf)
        m = jnp.maximum(jnp.maximum(jnp.max(s_old, axis=-1, keepdims=True),
                                    jnp.max(s_new, axis=-1, keepdims=True)), sink)
        e_old = jnp.exp(s_old - m)
        e_new = jnp.exp(s_new - m)
        denom = (jnp.sum(e_old, axis=-1, keepdims=True) + jnp.sum(e_new, axis=-1, keepdims=True)
                 + jnp.exp(sink - m))
        nt = (((1,), (1,)), ((), ()))
        o = (lax.dot_general(e_old.astype(BF16), vb.astype(BF16), nt, preferred_element_type=F32)
             + lax.dot_general(e_new.astype(BF16), vn.astype(BF16), nt, preferred_element_type=F32))
        o_ref[i] = o / denom
        keep = lane_k < wb - dt
        ko_ref[i] = jnp.where(keep, pltpu.roll(kb, wb - dt, 1), kn)
        vo_ref[i] = jnp.where(keep, pltpu.roll(vb, wb - dt, 1), vn)


def _swa_sample(qe, sink_rows, buf_kt, buf_vt, layer, knew, vnew, dt, past_len):
    db, r, _ = qe.shape
    wb = buf_kt.shape[-1]
    n_seq = 4 if db % 4 == 0 else 1
    seq_spec = lambda shape: pl.BlockSpec((n_seq,) + shape, lambda b: (b, 0, 0))
    buf_spec = pl.BlockSpec((None, n_seq, KV_DIM, wb), lambda b: (layer, b, 0, 0))
    return pl.pallas_call(
        functools.partial(_swa_sample_kernel, n_seq=n_seq, dt=dt, past_len=past_len),
        grid=(db // n_seq,),
        in_specs=[seq_spec((r, KV_DIM)), pl.BlockSpec((r, LANES), lambda b: (0, 0)),
                  buf_spec, buf_spec, seq_spec((KV_DIM, wb)), seq_spec((KV_DIM, wb))],
        out_specs=[seq_spec((r, KV_DIM)), seq_spec((KV_DIM, wb)), seq_spec((KV_DIM, wb))],
        out_shape=[jax.ShapeDtypeStruct((db, r, KV_DIM), F32),
                   jax.ShapeDtypeStruct((db, KV_DIM, wb), F32),
                   jax.ShapeDtypeStruct((db, KV_DIM, wb), F32)],
        compiler_params=_cparams(("arbitrary",)),
        name="swa_sample",
    )(qe, sink_rows, buf_kt, buf_vt, knew, vnew)


_GELU_C0 = 0.7978845608028654
_GELU_C1 = 0.7978845608028654 * 0.044715


def _gelu_tanh(x):
    t = jnp.tanh(x * (_GELU_C0 + _GELU_C1 * (x * x)))
    return x * (0.5 + 0.5 * t)


def _ffn_kernel(*refs, halo, stride, multi_tile, rb, tf, resident):
    if resident:
        (o_ref, x_ref, gm_ref, sh_ref, sc_ref, gf_ref, gpm_ref, gpre_ref, gpost_ref, wo_ref,
         wup_ref, cw_ref, cb_ref, wd_ref, prev_ref, out_ref, tail_ref,
         x1_ref, h_ref, act_ref, buf0_ref, buf1_ref, carry_ref) = refs
    else:
        (o_ref, x_ref, gm_ref, sh_ref, sc_ref, gf_ref, gpm_ref, gpre_ref, gpost_ref, wo_ref,
         wg_ref, wv_ref, cwg_ref, cwv_ref, cbg_ref, cbv_ref, wd_ref, pg_ref, pv_ref, out_ref, tg_ref, tv_ref,
         x1_ref, h_ref, act_ref, buf0_ref, buf1_ref, carry_ref) = refs
    ti = pl.program_id(1)
    nc = act_ref.shape[0]
    dff = wd_ref.shape[0]
    tm = x_ref.shape[0]
    bufs = (buf0_ref, buf1_ref)

    def pair(gate_val):
        return jnp.concatenate(gate_val, axis=1)

    def chunk_cols(cc, full_ref, g_ref, v_ref):
        if resident:
            return pair([full_ref[:, tf * cc:tf * (cc + 1)], full_ref[:, dff + tf * cc:dff + tf * (cc + 1)]])
        return pair([g_ref[...], v_ref[...]])

    def project(cc, buf):
        prev = (chunk_cols(cc, prev_ref, None, None) if resident else chunk_cols(cc, None, pg_ref, pv_ref))
        if multi_tile:
            first = jnp.full((halo, 2 * tf), ti, jnp.int32) == 0
            prev = jnp.where(first, prev, carry_ref[cc])
        buf[0:halo, :] = prev
        w = chunk_cols(cc, wup_ref, None, None) if resident else chunk_cols(cc, None, wg_ref, wv_ref)
        buf[halo:halo + tm, :] = jnp.dot(h_ref[...], w, preferred_element_type=F32)
        last = buf[tm:tm + halo, :]
        if multi_tile:
            carry_ref[cc] = last
        if resident:
            tail_ref[:, tf * cc:tf * (cc + 1)] = last[:, :tf]
            tail_ref[:, dff + tf * cc:dff + tf * (cc + 1)] = last[:, tf:]
        else:
            tg_ref[...] = last[:, :tf]
            tv_ref[...] = last[:, tf:]

    def activate(cc, buf):
        cw = chunk_cols(cc, cw_ref, None, None) if resident else chunk_cols(cc, None, cwg_ref, cwv_ref)
        cb = chunk_cols(cc, cb_ref, None, None) if resident else chunk_cols(cc, None, cbg_ref, cbv_ref)
        for r0 in range(0, tm, rb):
            u0 = buf[halo + r0:halo + r0 + rb, :]
            u1 = buf[halo - stride + r0:halo - stride + r0 + rb, :]
            u2 = buf[halo - 2 * stride + r0:halo - 2 * stride + r0 + rb, :]
            conv = cb + cw[0:1] * u2 + cw[1:2] * u1 + cw[2:3] * u0
            act_ref[cc, r0:r0 + rb, :] = (_gelu_tanh(conv[:, :tf]) * conv[:, tf:]).astype(BF16)

    halves = [(0, tm)] if tm % 16 else [(0, tm // 2), (tm // 2, tm)]

    def mod_rows(ref, lo, hi):
        m = ref[...]
        return m if m.shape[0] == 1 else _rows(m, tm)[lo:hi]

    def prologue():
        ys = [jnp.dot(o_ref[lo:hi, :], wo_ref[...], preferred_element_type=F32) for lo, hi in halves]
        for (lo, hi), y in zip(halves, ys):
            x1 = x_ref[lo:hi, :] + mod_rows(gm_ref, lo, hi) * _rms(y, gpm_ref[...])
            x1_ref[lo:hi, :] = x1
            h_ref[lo:hi, :] = (_rms(x1, gpre_ref[...]) * (1.0 + mod_rows(sc_ref, lo, hi))
                               + mod_rows(sh_ref, lo, hi)).astype(BF16)

    def epilogue():
        kd = (nc - 1) * tf
        for lo, hi in halves:
            y = jnp.dot(act_ref[nc - 1, lo:hi, :], wd_ref[kd:, :], preferred_element_type=F32)
            if nc > 1:
                early = jnp.concatenate([act_ref[k, lo:hi, :] for k in range(nc - 1)], axis=1)
                y = y + jnp.dot(early, wd_ref[:kd, :], preferred_element_type=F32)
            out_ref[lo:hi, :] = x1_ref[lo:hi, :] + mod_rows(gf_ref, lo, hi) * _rms(y, gpost_ref[...])

    if resident:
        prologue()
        project(0, bufs[0])
        for cc in range(1, nc):
            activate(cc - 1, bufs[(cc - 1) % 2])
            project(cc, bufs[cc % 2])
        activate(nc - 1, bufs[(nc - 1) % 2])
        epilogue()
        return

    c = pl.program_id(2)

    @pl.when(c == 0)
    def _():
        prologue()
        project(c, bufs[0])

    for parity in range(2):
        @pl.when((c > 0) & (c < nc) & (lax.rem(c, 2) == parity))
        def _():
            activate(c - 1, bufs[1 - parity])
            project(c, bufs[parity])

    @pl.when(c == nc)
    def _():
        activate(c - 1, bufs[(nc - 1) % 2])
        epilogue()


def _ffn(o, x, gate_m, shift, scale, gate_f, gpost_m, gpre, gpost, wo, w_up, cw, cb, wd, prev,
         *, tm, tf, halo, stride, mod_map, resident):
    nb, r, d = x.shape
    dff = wd.shape[0]
    nc = dff // tf
    nt = r // tm
    mr = shift.shape[1]
    multi_tile = nt > 1
    rb = 64 if tm % 64 == 0 else tm
    once = pl.Buffered(1)
    row_mode = dict(pipeline_mode=once) if nb * nt == 1 else {}
    kern = functools.partial(_ffn_kernel, halo=halo, stride=stride, multi_tile=multi_tile, rb=rb, tf=tf,
                             resident=resident)
    scratch = [
        pltpu.VMEM((tm, d), F32), pltpu.VMEM((tm, d), BF16), pltpu.VMEM((nc, tm, tf), BF16),
        pltpu.VMEM((halo + tm, 2 * tf), F32), pltpu.VMEM((halo + tm, 2 * tf), F32),
        pltpu.VMEM((nc, halo, 2 * tf) if multi_tile else (1, 8, LANES), F32),
    ]
    out_sds = jax.ShapeDtypeStruct((nb, r, d), F32)
    if resident:
        im = lambda b, t: (b, t, 0)
        seq = lambda b, t: (b, 0, 0)
        const = lambda b, t: (0, 0)
        mod_spec = pl.BlockSpec((None, mr, d), lambda b, t: (mod_map(b), 0, 0))
        vec_spec = pl.BlockSpec((1, d), const)
        out, tail = pl.pallas_call(
            kern, grid=(nb, nt),
            in_specs=[
                pl.BlockSpec((None, tm, Q_DIM), im, **row_mode), pl.BlockSpec((None, tm, d), im, **row_mode),
                mod_spec, mod_spec, mod_spec, mod_spec, vec_spec, vec_spec, vec_spec,
                pl.BlockSpec(wo.shape, const, pipeline_mode=once),
                pl.BlockSpec(w_up.shape, const, pipeline_mode=once),
                pl.BlockSpec(cw.shape, const), pl.BlockSpec(cb.shape, const),
                pl.BlockSpec(wd.shape, const, pipeline_mode=once),
                pl.BlockSpec((None, halo, 2 * dff), seq),
            ],
            out_specs=[pl.BlockSpec((None, tm, d), im), pl.BlockSpec((None, halo, 2 * dff), seq)],
            out_shape=[out_sds, jax.ShapeDtypeStruct((nb, halo, 2 * dff), F32)],
            scratch_shapes=scratch,
            compiler_params=_cparams(("arbitrary", "arbitrary")),
            name="mixer_out_conv_ffn",
        )(o, x, gate_m, shift, scale, gate_f, gpost_m, gpre, gpost, wo, w_up, cw, cb, wd, prev)
        return out, tail

    assert not multi_tile
    last = nc - 1
    im = lambda b, t, c: (b, t, 0)
    const = lambda b, t, c: (0, 0)
    mod_spec = pl.BlockSpec((None, mr, d), lambda b, t, c: (mod_map(b), 0, 0))
    vec_spec = pl.BlockSpec((1, d), const)
    proj = lambda rows, off: pl.BlockSpec((rows, tf), lambda b, t, c: (0, off + jnp.minimum(c, last)))
    actv = lambda rows, off: pl.BlockSpec((rows, tf), lambda b, t, c: (0, off + jnp.maximum(c - 1, 0)))
    halo_spec = lambda off: pl.BlockSpec((None, halo, tf), lambda b, t, c: (b, 0, off + jnp.minimum(c, last)))
    tail_sds = jax.ShapeDtypeStruct((nb, halo, dff), F32)
    out, tail_g, tail_v = pl.pallas_call(
        kern, grid=(nb, nt, nc + 1),
        in_specs=[
            pl.BlockSpec((None, tm, Q_DIM), im, **row_mode), pl.BlockSpec((None, tm, d), im, **row_mode),
            mod_spec, mod_spec, mod_spec, mod_spec, vec_spec, vec_spec, vec_spec,
            pl.BlockSpec(wo.shape, const, pipeline_mode=once),
            proj(d, 0), proj(d, nc), actv(3, 0), actv(3, nc), actv(1, 0), actv(1, nc),
            pl.BlockSpec(wd.shape, const, pipeline_mode=once),
            halo_spec(0), halo_spec(nc),
        ],
        out_specs=[pl.BlockSpec((None, tm, d), im), halo_spec(0), halo_spec(0)],
        out_shape=[out_sds, tail_sds, tail_sds],
        scratch_shapes=scratch,
        compiler_params=_cparams(("arbitrary", "arbitrary", "arbitrary")),
        name="mixer_out_conv_ffn_chunked",
    )(o, x, gate_m, shift, scale, gate_f, gpost_m, gpre, gpost, wo, w_up, w_up, cw, cw, cb, cb, wd, prev, prev)
    return out, jnp.concatenate([tail_g, tail_v], axis=-1)


def _rope_tables(pos):
    inv_freq = jnp.power(jnp.float32(ROPE_THETA), -jnp.arange(ROT_HALF, dtype=F32) * (2.0 / ROT_DIM))
    ang = pos.astype(F32)[:, None] * inv_freq[None, :]
    cos, sin = jnp.cos(ang), jnp.sin(ang)
    n = pos.shape[0]
    ones = jnp.ones((n, HEAD_DIM - ROT_DIM), F32)
    zeros_r = jnp.zeros((n, HEAD_DIM - ROT_DIM), F32)
    zeros_h = jnp.zeros((n, ROT_HALF), F32)
    cq = jnp.concatenate([cos, cos, ones], axis=1)
    s1 = jnp.concatenate([zeros_h, sin, zeros_r], axis=1)
    s2 = jnp.concatenate([-sin, zeros_h, zeros_r], axis=1)
    rep = LANES // HEAD_DIM
    return (jnp.tile(cq, (1, rep)), jnp.tile(s1, (1, rep)), jnp.tile(s2, (1, rep)), cos.T, sin.T)


def _kv_onehot():
    oh = np.zeros((N_QBLK, 2, N_KV_HEADS), np.float32)
    for i in range(N_QBLK):
        for p in range(2):
            oh[i, p, 2 * (i // GROUP) + p] = 1.0
    return oh


def _expand_q(q_tm):
    dt, db, _ = q_tm.shape
    q5 = q_tm.reshape(dt, db, N_QBLK, 2, HEAD_DIM).transpose(1, 2, 3, 0, 4)
    oh = jnp.asarray(_kv_onehot(), q_tm.dtype)
    q6 = q5[:, :, :, :, None, :] * oh[None, :, :, None, :, None]
    return q6.reshape(db, N_QBLK * 2 * dt, KV_DIM)


def _select_o(o_full, dt):
    db = o_full.shape[0]
    o6 = o_full.reshape(db, N_QBLK, 2, dt, N_KV_HEADS, HEAD_DIM)
    o5 = jnp.sum(o6 * jnp.asarray(_kv_onehot())[None, :, :, None, :, None], axis=4)
    return o5.transpose(3, 0, 1, 2, 4).reshape(dt, db, Q_DIM).astype(BF16)


def _rows_of_heads(v, dt):
    vp = v[jnp.asarray(HEAD_PERM)]
    return jnp.broadcast_to(vp[:, None, None], (N_HEADS, dt, LANES)).reshape(N_HEADS * dt, LANES).astype(F32)


def _from_t(kt, lead):
    n = kt.shape[-1]
    x = kt.reshape(lead + (N_KV_HEADS, HEAD_DIM, n))
    nd = len(lead)
    return x.transpose(tuple(range(nd)) + (nd + 2, nd, nd + 1))


def kernel(x_prompt, x_sample, cache_sb_k, cache_sb_v, state_swa_k, state_swa_v, state_conv, page_table,
           c_prompt, c_sample, mod_w, mod_b, norm_mix_pre, norm_mix_post, norm_ffn_pre, norm_ffn_post,
           sb_w_qkv, sb_w_o, sb_bias, swa_w_qkv, swa_w_o, swa_sinks, ffn_w_up, ffn_conv_w, ffn_conv_b,
           ffn_w_down):
    b, t, d = x_prompt.shape
    db, dt, _ = x_sample.shape
    depth = mod_w.shape[0]
    n_pages = page_table.shape[1]
    past_len = n_pages * PAGE_SIZE
    wb = state_swa_k.shape[2]
    dff = ffn_w_down.shape[1]
    n_phys = cache_sb_k.shape[1]

    tm_p = min(512, t)
    tm_f = min(512, t)
    tf = 256

    mod = _modulation(jnp.concatenate([c_prompt, c_sample], axis=0), mod_w, mod_b)
    mod_p = mod[:, :, :b].reshape(depth, N_MOD, b, 1, d)
    mod_s = mod[:, :, b:].reshape(depth, N_MOD, 1, db, d)

    cache_kt = cache_sb_k.transpose(0, 1, 3, 4, 2).reshape(-1, n_phys, KV_DIM, PAGE_SIZE)
    cache_vt = cache_sb_v.transpose(0, 1, 3, 4, 2).reshape(-1, n_phys, KV_DIM, PAGE_SIZE)
    swa_kt = state_swa_k.transpose(0, 1, 3, 4, 2).reshape(-1, db, KV_DIM, wb)
    swa_vt = state_swa_v.transpose(0, 1, 3, 4, 2).reshape(-1, db, KV_DIM, wb)

    qcols = jnp.asarray(np.concatenate([np.arange(HEAD_DIM) + HEAD_DIM * h for h in HEAD_PERM]))
    perm = jnp.asarray(HEAD_PERM)

    rope_p = _rope_tables(jnp.arange(t))
    rope_p = tuple(a[None] for a in rope_p) + (False,)
    cq, s1, s2, ck, sk = _rope_tables(past_len + jnp.arange(dt))
    rope_s = (cq[:, None, :], s1[:, None, :], s2[:, None, :],
              jnp.broadcast_to(ck.T[:, :, None], (dt, ROT_HALF, db)),
              jnp.broadcast_to(sk.T[:, :, None], (dt, ROT_HALF, db)), True)

    xp = x_prompt
    xs = x_sample.transpose(1, 0, 2)
    pmap = lambda bi: bi
    smap = lambda bi: 0

    outs = {k: [] for k in ("sbkp", "sbvp", "sbks", "sbvs", "swkp", "swvp", "swks", "swvs", "cp", "cs")}
    for i in range(depth):
        j = i // 2
        is_sb = i % 2 == 0
        w_qkv = (sb_w_qkv if is_sb else swa_w_qkv)[j]
        w_o = (sb_w_o if is_sb else swa_w_o)[j]
        wq = w_qkv[:, :Q_DIM][:, qcols].astype(BF16)
        wkvt = w_qkv[:, Q_DIM:].T.astype(BF16)
        wo = w_o[qcols, :].astype(BF16)
        g_pre = norm_mix_pre[i][None]
        g_post = norm_mix_post[i][None]
        mp = [mod_p[i, k] for k in range(N_MOD)]
        ms = [mod_s[i, k] for k in range(N_MOD)]

        qp, ktp, vtp, kvbp = _qkv(xp, mp[0], mp[1], g_pre, wq, wkvt, None if is_sb else rope_p,
                                  tm=tm_p, mod_map=pmap)
        qs, kts, vts, _ = _qkv(xs, ms[0], ms[1], g_pre, wq, wkvt, None if is_sb else rope_s,
                               tm=db, mod_map=smap)
        qe = _expand_q(qs)
        kn = kts.transpose(2, 1, 0)
        vn = vts.transpose(2, 1, 0)
        if is_sb:
            op = _sb_prompt(qp, kvbp, sb_bias[j][perm])
            pad = ((0, 0), (0, 0), (0, LANES - dt))
            o_full = _sb_sample(page_table, qe, _rows_of_heads(sb_bias[j], dt),
                                jnp.pad(kn, pad).astype(BF16), jnp.pad(vn, pad).astype(BF16),
                                cache_kt, cache_vt, j, dt)
            outs["sbkp"].append(_from_t(ktp, (b,)))
            outs["sbvp"].append(_from_t(vtp, (b,)))
            outs["sbks"].append(_from_t(kn, (db,)))
            outs["sbvs"].append(_from_t(vn, (db,)))
        else:
            op = _swa_prompt(qp, kvbp, swa_sinks[j][perm])
            pad = ((0, 0), (0, 0), (wb - dt, 0))
            o_full, ko, vo = _swa_sample(qe, _rows_of_heads(swa_sinks[j], dt), swa_kt, swa_vt, j,
                                         jnp.pad(kn, pad), jnp.pad(vn, pad), dt, past_len)
            keep = min(WINDOW, t)
            outs["swkp"].append(_from_t(ktp[:, :, t - keep:], (b,)))
            outs["swvp"].append(_from_t(vtp[:, :, t - keep:], (b,)))
            outs["swks"].append(_from_t(ko, (db,)))
            outs["swvs"].append(_from_t(vo, (db,)))
        os_ = _select_o(o_full, dt)

        w_up = ffn_w_up[i].astype(BF16)
        wd = ffn_w_down[i].astype(BF16)
        cw, cb = ffn_conv_w[i], ffn_conv_b[i][None]
        gf_pre = norm_ffn_pre[i][None]
        gf_post = norm_ffn_post[i][None]

        halo_p = 8
        zero_prev = jnp.zeros((b, halo_p, 2 * dff), F32)
        xp, tail_p = _ffn(op, xp, mp[2], mp[3], mp[4], mp[5], g_post, gf_pre, gf_post, wo,
                          w_up, cw, cb, wd, zero_prev,
                          tm=tm_f, tf=tf, halo=halo_p, stride=1, mod_map=pmap, resident=True)
        st = state_conv[i].transpose(1, 0, 2).reshape(1, 2 * db, 2 * dff)
        xs1, tail_s = _ffn(os_.reshape(1, dt * db, Q_DIM), xs.reshape(1, dt * db, d),
                           ms[2], ms[3], ms[4], ms[5], g_post, gf_pre, gf_post, wo,
                           w_up, cw, cb, wd, st,
                           tm=dt * db, tf=tf, halo=2 * db, stride=db, mod_map=smap, resident=False)
        xs = xs1.reshape(dt, db, d)
        outs["cp"].append(tail_p[:, halo_p - 2:])
        outs["cs"].append(tail_s.reshape(2, db, 2 * dff).transpose(1, 0, 2))

    st_ = lambda k: jnp.stack(outs[k])
    return (xp, xs.transpose(1, 0, 2),
            st_("sbkp"), st_("sbvp"), st_("sbks"), st_("sbvs"),
            st_("swkp"), st_("swvp"), st_("swks"), st_("swvs"),
            st_("cp"), st_("cs"))
```

```python
import functools

import numpy as np
import jax
import jax.numpy as jnp
from jax import lax
from jax.experimental import pallas as pl
from jax.experimental.pallas import tpu as pltpu

F32 = jnp.float32
BF16 = jnp.bfloat16

HEAD_DIM = 64
N_HEADS = 16
N_KV_HEADS = 4
GROUP = N_HEADS // N_KV_HEADS
Q_DIM = N_HEADS * HEAD_DIM
KV_DIM = N_KV_HEADS * HEAD_DIM
PAGE_SIZE = 128
WINDOW = 128
Q_BLOCK = 128
ROT_DIM = HEAD_DIM // 4
ROT_HALF = ROT_DIM // 2
ROPE_THETA = 500000.0
RMS_EPS = 1e-6
N_MOD = 6
LANES = 128
N_QBLK = Q_DIM // LANES
N_PAIR = KV_DIM // LANES
VMEM_LIMIT = 56 * 1024 * 1024

HEAD_PERM = [4 * (2 * (i // 4) + p) + (i % 4) for i in range(N_QBLK) for p in range(2)]


def _cparams(sem):
    return pltpu.CompilerParams(dimension_semantics=sem, vmem_limit_bytes=VMEM_LIMIT)


def _rms(x, g):
    return x * lax.rsqrt(jnp.mean(x * x, axis=-1, keepdims=True) + RMS_EPS) * g


def _rows(m, rows):
    mr = m.shape[0]
    if mr == 1 or mr == rows:
        return m
    return jnp.concatenate([m] * (rows // mr), axis=0)


def _norm_mod(x, g, shift, scale):
    rows = x.shape[0]
    return _rms(x, g) * (1.0 + _rows(scale, rows)) + _rows(shift, rows)


def _slab(param, tail_block, tail_index, **kw):
    _, lead = param
    lead = tuple(lead)
    return pl.BlockSpec((None,) * len(lead) + tuple(tail_block),
                        lambda *grid: lead + tuple(tail_index(*grid)), **kw)


def _mod_kernel(c_ref, w_ref, b_ref, o_ref):
    c = c_ref[...]
    s = c / (1.0 + jnp.exp(-c))
    o_ref[...] = jnp.dot(s.astype(BF16), w_ref[...].astype(BF16), preferred_element_type=F32) + b_ref[...]


def _modulation(c_all, mod_w, mod_b):
    depth, d, _ = mod_w.shape
    n = c_all.shape[0]
    return pl.pallas_call(
        _mod_kernel,
        grid=(depth, N_MOD),
        in_specs=[
            pl.BlockSpec((n, d), lambda i, k: (0, 0)),
            pl.BlockSpec((None, d, d), lambda i, k: (i, 0, k)),
            pl.BlockSpec((None, None, 1, d), lambda i, k: (i, k, 0, 0)),
        ],
        out_specs=pl.BlockSpec((None, None, n, d), lambda i, k: (i, k, 0, 0)),
        out_shape=jax.ShapeDtypeStruct((depth, N_MOD, n, d), F32),
        compiler_params=_cparams(("arbitrary", "arbitrary")),
        name="modulation",
    )(c_all, mod_w, mod_b.reshape(depth, N_MOD, 1, d))


def _qkv_kernel(*refs, rope, kc):
    if rope:
        (x_ref, sh_ref, sc_ref, g_ref, wq_ref, wkv_ref, cq_ref, s1_ref, s2_ref, ck_ref, sk_ref,
         q_ref, kt_ref, vt_ref, kvb_ref) = refs
    else:
        x_ref, sh_ref, sc_ref, g_ref, wq_ref, wkv_ref, q_ref, kt_ref, vt_ref, kvb_ref = refs
    tm = x_ref.shape[0]
    h = _norm_mod(x_ref[...], g_ref[...], sh_ref[...], sc_ref[...]).astype(BF16)
    q = jnp.dot(h, wq_ref[...], preferred_element_type=F32)
    kvt = lax.dot_general(wkv_ref[...], h, (((1,), (1,)), ((), ())),
                          preferred_element_type=F32)
    scale = HEAD_DIM ** -0.5
    if rope:
        cq, s1, s2 = cq_ref[...], s1_ref[...], s2_ref[...]
        for i in range(N_QBLK):
            xr = q[:, LANES * i:LANES * (i + 1)]
            rot = xr * cq + pltpu.roll(xr, ROT_HALF, 1) * s1 + pltpu.roll(xr, LANES - ROT_HALF, 1) * s2
            q_ref[:, LANES * i:LANES * (i + 1)] = (rot * scale).astype(BF16)
        ck, sk = ck_ref[...], sk_ref[...]
        pieces = []
        for c in range(N_KV_HEADS):
            base = HEAD_DIM * c
            x1 = kvt[base:base + ROT_HALF]
            x2 = kvt[base + ROT_HALF:base + ROT_DIM]
            pieces += [x1 * ck - x2 * sk, x2 * ck + x1 * sk, kvt[base + ROT_DIM:base + HEAD_DIM]]
        kt = jnp.concatenate(pieces, axis=0)
    else:
        q_ref[...] = (q * scale).astype(BF16)
        kt = kvt[:KV_DIM]
    vt = kvt[KV_DIM:]
    kt_ref[...] = kt
    vt_ref[...] = vt
    kb = kt.astype(BF16)
    vb = vt.astype(BF16)
    for c in range(tm // kc):
        kvb_ref[c, :KV_DIM, :] = kb[:, kc * c:kc * (c + 1)]
        kvb_ref[c, KV_DIM:, :] = vb[:, kc * c:kc * (c + 1)]


def _qkv(x, shift, scale, g, wq, wkvt, rope_tabs, *, tm, mod_map):
    nb, r, d = x.shape
    kc = min(LANES, tm)
    nt = r // tm
    rope = rope_tabs is not None
    mr = shift[0].shape[-2]
    const = lambda b, t: (0, 0)
    mod_index = lambda b, t: (mod_map(b), 0, 0)
    in_specs = [
        pl.BlockSpec((None, tm, d), lambda b, t: (b, t, 0)),
        _slab(shift, (None, mr, d), mod_index),
        _slab(scale, (None, mr, d), mod_index),
        _slab(g, (1, d), const),
        _slab(wq, wq[0].shape[-2:], const),
        _slab(wkvt, wkvt[0].shape[-2:], const),
    ]
    args = [x, shift[0], scale[0], g[0], wq[0], wkvt[0]]
    if rope:
        cq, s1, s2, ck, sk, per_b = rope_tabs
        if per_b:
            qspec = pl.BlockSpec((None, 1, LANES), lambda b, t: (b, 0, 0))
            kspec = pl.BlockSpec((None, ROT_HALF, tm), lambda b, t: (b, 0, 0))
        else:
            qspec = pl.BlockSpec((None, tm, LANES), lambda b, t: (0, t, 0))
            kspec = pl.BlockSpec((None, ROT_HALF, tm), lambda b, t: (0, 0, t))
        in_specs += [qspec, qspec, qspec, kspec, kspec]
        args += [cq, s1, s2, ck, sk]
    out_shape = [
        jax.ShapeDtypeStruct((nb, r, Q_DIM), BF16),
        jax.ShapeDtypeStruct((nb, KV_DIM, r), F32),
        jax.ShapeDtypeStruct((nb, KV_DIM, r), F32),
        jax.ShapeDtypeStruct((nb, r // kc, 2 * KV_DIM, kc), BF16),
    ]
    out_specs = [
        pl.BlockSpec((None, tm, Q_DIM), lambda b, t: (b, t, 0)),
        pl.BlockSpec((None, KV_DIM, tm), lambda b, t: (b, 0, t)),
        pl.BlockSpec((None, KV_DIM, tm), lambda b, t: (b, 0, t)),
        pl.BlockSpec((None, tm // kc, 2 * KV_DIM, kc), lambda b, t: (b, t, 0, 0)),
    ]
    return pl.pallas_call(
        functools.partial(_qkv_kernel, rope=rope, kc=kc),
        grid=(nb, nt),
        in_specs=in_specs,
        out_specs=out_specs,
        out_shape=out_shape,
        compiler_params=_cparams(("arbitrary", "arbitrary")),
        name="qkv_rope" if rope else "qkv",
    )(*args)


LOG2E = 1.4426950408889634


def _sb_split(z, mask):
    n = z.shape[1] // LANES
    sp = jnp.maximum(z, 0.0) + jnp.log(1.0 + jnp.exp2(jnp.abs(z) * (-LOG2E)))
    if mask is not None:
        sp = jnp.where(mask, sp, 0.0)
    hi_f = lax.bitcast_convert_type(
        lax.bitcast_convert_type(sp, jnp.uint32) & jnp.uint32(0xFFFF0000), F32)
    hi = hi_f.astype(BF16)
    lo = (sp - hi_f).astype(BF16)
    return jnp.concatenate(
        [jnp.concatenate([hi[:, LANES * k:LANES * (k + 1)], lo[:, LANES * k:LANES * (k + 1)]], axis=1)
         for k in range(n)], axis=0)


def _sb_log_weights(z, t, carry):
    r = z.shape[0]
    n = z.shape[1] // LANES
    tails = [None] * n
    for k in reversed(range(n)):
        blk = t[r * k:r * (k + 1)]
        tails[k] = blk[:, :LANES] + carry
        carry = carry + blk[:, LANES:]
    tail = tails[0] if n == 1 else jnp.concatenate(tails, axis=1)
    return z + tail, carry


def _sb_finish(z, t, mask, carry):
    lw, carry = _sb_log_weights(z, t, carry)
    a = jnp.exp(lw)
    if mask is not None:
        a = jnp.where(mask, a, 0.0)
    return a.astype(BF16), carry


def _cumsum_total_matrix():
    j = np.arange(LANES)[:, None]
    c = np.arange(LANES)[None, :]
    half = np.concatenate([-(j >= c).astype(np.float32), -np.ones((LANES, LANES), np.float32)], axis=1)
    return jnp.asarray(np.concatenate([half, half], axis=0), dtype=BF16)


def _sb_prompt_kernel(bias_ref, q_ref, kvb_ref, w_ref, o_ref, *scratch):
    qi = pl.program_id(1)
    rows = GROUP * Q_BLOCK
    lane = lax.broadcasted_iota(jnp.int32, (Q_BLOCK, LANES), 1)
    top = qi // 2
    qs_refs, c_refs, acc_refs = scratch[0:4], scratch[4:8], scratch[8:12]
    chains = [(j, p) for j in range(N_PAIR) for p in range(2)]
    nt_dims = (((1,), (1,)), ((), ()))
    for n, (j, p) in enumerate(chains):
        in_half = (lane // HEAD_DIM) == p
        for g in range(GROUP):
            blk = q_ref[:, LANES * (GROUP * j + g):LANES * (GROUP * j + g + 1)]
            qs_refs[n][Q_BLOCK * g:Q_BLOCK * (g + 1), :] = jnp.where(in_half, blk, 0)
        c_refs[n][...] = jnp.zeros_like(c_refs[n])
        acc_refs[n][...] = jnp.zeros_like(acc_refs[n])

    def chunk(ci, masked):
        kts = [jnp.concatenate([kvb_ref[2 * ci, LANES * j:LANES * (j + 1), :],
                                kvb_ref[2 * ci + 1, LANES * j:LANES * (j + 1), :]], axis=1)
               for j in range(N_PAIR)]
        vts = [jnp.concatenate([kvb_ref[2 * ci, KV_DIM + LANES * j:KV_DIM + LANES * (j + 1), :],
                                kvb_ref[2 * ci + 1, KV_DIM + LANES * j:KV_DIM + LANES * (j + 1), :]], axis=1)
               for j in range(N_PAIR)]
        mask = None
        if masked:
            r_idx = lax.broadcasted_iota(jnp.int32, (rows, 2 * LANES), 0) & (Q_BLOCK - 1)
            c_idx = lax.broadcasted_iota(jnp.int32, (rows, 2 * LANES), 1)
            mask = (ci * 2 * LANES + c_idx) < (qi * Q_BLOCK + r_idx)
        zs = [jnp.dot(qs_refs[n][...], kts[j], preferred_element_type=F32) for n, (j, p) in enumerate(chains)]
        zs = [jnp.concatenate(
            [zs[n][Q_BLOCK * g:Q_BLOCK * (g + 1)] + bias_ref[2 * (GROUP * j + g) + p]
             for g in range(GROUP)], axis=0) for n, (j, p) in enumerate(chains)]
        hls = [_sb_split(z, mask) for z in zs]
        ts = [jnp.dot(hl, w_ref[...], preferred_element_type=F32) for hl in hls]
        aw = [_sb_finish(zs[n], ts[n], mask, c_refs[n][...]) for n in range(len(chains))]
        for n in range(len(chains)):
            c_refs[n][...] = aw[n][1]
        pv = [lax.dot_general(aw[n][0], vts[j], nt_dims, preferred_element_type=F32)
              for n, (j, p) in enumerate(chains)]
        for n in range(len(chains)):
            acc_refs[n][...] += pv[n]

    chunk(top, True)

    def body(n, _):
        chunk(top - 1 - n, False)
        return 0

    lax.fori_loop(0, top, body, 0)
    lane_r = lax.broadcasted_iota(jnp.int32, (rows, LANES), 1)
    for j in range(N_PAIR):
        o_pair = jnp.where((lane_r // HEAD_DIM) == 0, acc_refs[2 * j][...],
                           acc_refs[2 * j + 1][...]).astype(BF16)
        for g in range(GROUP):
            o_ref[:, LANES * (GROUP * j + g):LANES * (GROUP * j + g + 1)] = \
                o_pair[Q_BLOCK * g:Q_BLOCK * (g + 1)]


def _sb_prompt(q, kvb, bias_perm):
    b, t, _ = q.shape
    nq = t // Q_BLOCK
    nck = kvb.shape[1]
    rows = GROUP * Q_BLOCK
    return pl.pallas_call(
        _sb_prompt_kernel,
        grid_spec=pltpu.PrefetchScalarGridSpec(
            num_scalar_prefetch=1,
            grid=(b, nq),
            in_specs=[
                pl.BlockSpec((None, Q_BLOCK, Q_DIM), lambda bi, qi, s: (bi, qi, 0)),
                pl.BlockSpec((None, nck, 2 * KV_DIM, LANES), lambda bi, qi, s: (bi, 0, 0, 0)),
                pl.BlockSpec((2 * LANES, 2 * LANES), lambda bi, qi, s: (0, 0)),
            ],
            out_specs=pl.BlockSpec((None, Q_BLOCK, Q_DIM), lambda bi, qi, s: (bi, qi, 0)),
            scratch_shapes=([pltpu.VMEM((rows, LANES), BF16)] * N_KV_HEADS
                            + [pltpu.VMEM((rows, LANES), F32)] * (2 * N_KV_HEADS)),
        ),
        out_shape=jax.ShapeDtypeStruct((b, t, Q_DIM), BF16),
        compiler_params=_cparams(("arbitrary", "arbitrary")),
        name="sb_prompt",
    )(bias_perm, q, kvb, _cumsum_total_matrix())


def _sb_sample_kernel(*refs, n_pp, dt, steps, n_seq):
    qe_ref, bias_ref, w_ref, kn_ref, vn_ref = refs[1:6]
    k_refs = refs[6:6 + n_pp]
    v_refs = refs[6 + n_pp:6 + 2 * n_pp]
    o_ref, c_ref, acc_ref, lw0_ref, lw1_ref = refs[6 + 2 * n_pp:]
    lws = (lw0_ref, lw1_ref)
    g = pl.program_id(0)
    last = n_seq * steps
    seq_a = g // steps
    seq_b = (g - 1) // steps
    r = qe_ref.shape[0]
    nt_dims = (((1,), (1,)), ((), ()))
    halves = 2 if n_pp % 2 == 0 else 1

    def logits():
        qe, bias = qe_ref[...], bias_ref[...]
        return jnp.concatenate(
            [jnp.dot(qe, k_refs[k][...].astype(BF16), preferred_element_type=F32) + bias
             for k in range(n_pp)], axis=1)

    def log_weights(z, lw_ref):
        hl = _sb_split(z, None)
        hr = hl.shape[0] // halves
        t = jnp.concatenate([jnp.dot(hl[i:i + hr], w_ref[...], preferred_element_type=F32)
                             for i in range(0, hl.shape[0], hr)], axis=0)
        lw, c_new = _sb_log_weights(z, t, c_ref[...])
        c_ref[...] = c_new
        lw_ref[...] = lw

    def weights_v(a):
        hk = n_pp // halves
        pv = None
        for k0 in range(0, n_pp, hk):
            vt = jnp.concatenate([v_refs[k][...].astype(BF16) for k in range(k0, k0 + hk)], axis=1)
            part = lax.dot_general(a[:, PAGE_SIZE * k0:PAGE_SIZE * (k0 + hk)], vt, nt_dims,
                                   preferred_element_type=F32)
            pv = part if pv is None else pv + part
        return pv

    def accumulate(pv):
        slot = lax.rem(seq_b, 2)
        acc = acc_ref[slot] + pv
        acc_ref[slot] = acc
        o_ref[...] = acc

    @pl.when((lax.rem(g, steps) == 0) & (g < last))
    def _():
        t_row = lax.broadcasted_iota(jnp.int32, (r, LANES), 0) % dt
        t_key = lax.broadcasted_iota(jnp.int32, (r, LANES), 1)
        z = jnp.dot(qe_ref[...], kn_ref[...], preferred_element_type=F32) + bias_ref[...]
        mask = t_key < t_row
        t = jnp.dot(_sb_split(z, mask), w_ref[...], preferred_element_type=F32)
        a, c_new = _sb_finish(z, t, mask, jnp.zeros((r, LANES), F32))
        c_ref[...] = c_new
        acc_ref[lax.rem(seq_a, 2)] = lax.dot_general(a, vn_ref[...], nt_dims, preferred_element_type=F32)

    @pl.when(g == 0)
    def _():
        log_weights(logits(), lws[0])

    for parity in range(2):
        @pl.when((g > 0) & (g < last) & (lax.rem(g, 2) == parity))
        def _():
            a = jnp.exp(lws[1 - parity][...]).astype(BF16)
            z = logits()
            pv = weights_v(a)
            log_weights(z, lws[parity])
            accumulate(pv)

    @pl.when(g == last)
    def _():
        accumulate(weights_v(jnp.exp(lws[(last - 1) % 2][...]).astype(BF16)))


def _sb_sample(page_table, qe, bias_rows, knew, vnew, cache_kt, cache_vt, layer, dt):
    db, r, _ = qe.shape
    n_pages = page_table.shape[1]
    n_pp = min(16, n_pages)
    while n_pages % n_pp:
        n_pp -= 1
    steps = n_pages // n_pp
    last = db * steps

    def page_spec(k, lag):
        def index(g, pt):
            gg = jnp.clip(g - lag, 0, last - 1)
            return (layer, pt[gg // steps, n_pages - n_pp * (gg % steps + 1) + k], 0, 0)
        return pl.BlockSpec((None, None, KV_DIM, PAGE_SIZE), index)

    seq_a = lambda g, pt: (jnp.minimum(g // steps, db - 1), 0, 0)
    in_specs = [
        pl.BlockSpec((None, r, KV_DIM), seq_a),
        pl.BlockSpec((r, LANES), lambda g, pt: (0, 0)),
        pl.BlockSpec((2 * LANES, 2 * LANES), lambda g, pt: (0, 0)),
        pl.BlockSpec((None, KV_DIM, LANES), seq_a),
        pl.BlockSpec((None, KV_DIM, LANES), seq_a),
    ] + [page_spec(k, 0) for k in range(n_pp)] + [page_spec(k, 1) for k in range(n_pp)]
    return pl.pallas_call(
        functools.partial(_sb_sample_kernel, n_pp=n_pp, dt=dt, steps=steps, n_seq=db),
        grid_spec=pltpu.PrefetchScalarGridSpec(
            num_scalar_prefetch=1,
            grid=(last + 1,),
            in_specs=in_specs,
            out_specs=pl.BlockSpec((None, r, KV_DIM), lambda g, pt: (jnp.maximum(g - 1, 0) // steps, 0, 0)),
            scratch_shapes=[pltpu.VMEM((r, LANES), F32), pltpu.VMEM((2, r, KV_DIM), F32),
                            pltpu.VMEM((r, n_pp * PAGE_SIZE), F32), pltpu.VMEM((r, n_pp * PAGE_SIZE), F32)],
        ),
        out_shape=jax.ShapeDtypeStruct((db, r, KV_DIM), F32),
        compiler_params=_cparams(("arbitrary",)),
        name="sb_sample",
    )(page_table, qe, bias_rows, _cumsum_total_matrix(), knew, vnew,
      *([cache_kt] * n_pp), *([cache_vt] * n_pp))


def _swa_prompt_kernel(sink_ref, q_ref, kvb_ref, o_ref, half_ref):
    qi = pl.program_id(1)
    rows = GROUP * Q_BLOCK
    lane = lax.broadcasted_iota(jnp.int32, (Q_BLOCK, LANES), 1)
    r_idx = lax.broadcasted_iota(jnp.int32, (rows, 2 * LANES), 0) & (Q_BLOCK - 1)
    c_idx = lax.broadcasted_iota(jnp.int32, (rows, 2 * LANES), 1)
    first_col = jnp.where(qi > 0, 0, WINDOW)
    mask = (c_idx >= r_idx) & (c_idx <= r_idx + WINDOW) & (c_idx >= first_col)
    prev = jnp.maximum(qi - 1, 0)
    ones = jnp.ones((LANES, 2 * LANES), BF16)
    for j in range(N_PAIR):
        kt = jnp.concatenate([kvb_ref[prev, LANES * j:LANES * (j + 1), :],
                              kvb_ref[qi, LANES * j:LANES * (j + 1), :]], axis=1)
        vt = jnp.concatenate([kvb_ref[prev, KV_DIM + LANES * j:KV_DIM + LANES * (j + 1), :],
                              kvb_ref[qi, KV_DIM + LANES * j:KV_DIM + LANES * (j + 1), :]], axis=1)
        vt_ones = jnp.concatenate([vt, ones], axis=0)
        for p in range(2):
            in_half = (lane // HEAD_DIM) == p
            qs = jnp.concatenate(
                [jnp.where(in_half, q_ref[:, LANES * (GROUP * j + g):LANES * (GROUP * j + g + 1)], 0)
                 for g in range(GROUP)], axis=0)
            sink = jnp.concatenate(
                [jnp.full((Q_BLOCK, LANES), sink_ref[2 * (GROUP * j + g) + p], F32) for g in range(GROUP)],
                axis=0)
            s = jnp.where(mask, jnp.dot(qs, kt, preferred_element_type=F32), -jnp.inf)
            m = jnp.maximum(jnp.max(s, axis=-1, keepdims=True), sink)
            e = jnp.exp(s - jnp.concatenate([m, m], axis=1)).astype(BF16)
            pv = lax.dot_general(e, vt_ones, (((1,), (1,)), ((), ())), preferred_element_type=F32)
            o = pv[:, :LANES] / (pv[:, LANES:] + jnp.exp(sink - m))
            if p == 0:
                half_ref[...] = o
            else:
                lane_r = lax.broadcasted_iota(jnp.int32, (rows, LANES), 1)
                o_pair = jnp.where((lane_r // HEAD_DIM) == 0, half_ref[...], o).astype(BF16)
                for g in range(GROUP):
                    o_ref[:, LANES * (GROUP * j + g):LANES * (GROUP * j + g + 1)] = \
                        o_pair[Q_BLOCK * g:Q_BLOCK * (g + 1)]


def _swa_prompt(q, kvb, sink_perm):
    b, t, _ = q.shape
    nq = t // Q_BLOCK
    nck = kvb.shape[1]
    return pl.pallas_call(
        _swa_prompt_kernel,
        grid_spec=pltpu.PrefetchScalarGridSpec(
            num_scalar_prefetch=1,
            grid=(b, nq),
            in_specs=[
                pl.BlockSpec((None, Q_BLOCK, Q_DIM), lambda bi, qi, s: (bi, qi, 0)),
                pl.BlockSpec((None, nck, 2 * KV_DIM, LANES), lambda bi, qi, s: (bi, 0, 0, 0)),
            ],
            out_specs=pl.BlockSpec((None, Q_BLOCK, Q_DIM), lambda bi, qi, s: (bi, qi, 0)),
            scratch_shapes=[pltpu.VMEM((GROUP * Q_BLOCK, LANES), F32)],
        ),
        out_shape=jax.ShapeDtypeStruct((b, t, Q_DIM), BF16),
        compiler_params=_cparams(("arbitrary", "arbitrary")),
        name="swa_prompt",
    )(sink_perm, q, kvb)


def _swa_sample_kernel(qe_ref, sink_ref, kb_ref, vb_ref, kn_ref, vn_ref, o_ref, ko_ref, vo_ref,
                       *, n_seq, dt, past_len):
    r = qe_ref.shape[1]
    wb = kb_ref.shape[2]
    t_row = lax.broadcasted_iota(jnp.int32, (r, wb), 0) % dt
    col = lax.broadcasted_iota(jnp.int32, (r, wb), 1)
    dist_old = wb + t_row - col
    mask_old = (dist_old <= WINDOW) & (col >= wb - past_len)
    mask_new = (col >= wb - dt) & (col - (wb - dt) <= t_row)
    lane_k = lax.broadcasted_iota(jnp.int32, (KV_DIM, wb), 1)
    sink = sink_ref[...][:, 0:1]
    for i in range(n_seq):
        qe = qe_ref[i]
        kb, vb, kn, vn = kb_ref[i], vb_ref[i], kn_ref[i], vn_ref[i]
        s_old = jnp.where(mask_old, jnp.dot(qe, kb.astype(BF16), preferred_element_type=F32), -jnp.inf)
        s_new = jnp.where(mask_new, jnp.dot(qe, kn.astype(BF16), preferred_element_type=F32), -jnp.inf)
        m = jnp.maximum(jnp.maximum(jnp.max(s_old, axis=-1, keepdims=True),
                                    jnp.max(s_new, axis=-1, keepdims=True)), sink)
        e_old = jnp.exp(s_old - m)
        e_new = jnp.exp(s_new - m)
        denom = (jnp.sum(e_old, axis=-1, keepdims=True) + jnp.sum(e_new, axis=-1, keepdims=True)
                 + jnp.exp(sink - m))
        nt = (((1,), (1,)), ((), ()))
        o = (lax.dot_general(e_old.astype(BF16), vb.astype(BF16), nt, preferred_element_type=F32)
             + lax.dot_general(e_new.astype(BF16), vn.astype(BF16), nt, preferred_element_type=F32))
        o_ref[i] = o / denom
        keep = lane_k < wb - dt
        ko_ref[i] = jnp.where(keep, pltpu.roll(kb, wb - dt, 1), kn)
        vo_ref[i] = jnp.where(keep, pltpu.roll(vb, wb - dt, 1), vn)


def _swa_sample(qe, sink_rows, buf_kt, buf_vt, layer, knew, vnew, dt, past_len):
    db, r, _ = qe.shape
    wb = buf_kt.shape[-1]
    n_seq = 4 if db % 4 == 0 else 1
    seq_spec = lambda shape: pl.BlockSpec((n_seq,) + shape, lambda b: (b, 0, 0))
    buf_spec = pl.BlockSpec((None, n_seq, KV_DIM, wb), lambda b: (layer, b, 0, 0))
    return pl.pallas_call(
        functools.partial(_swa_sample_kernel, n_seq=n_seq, dt=dt, past_len=past_len),
        grid=(db // n_seq,),
        in_specs=[seq_spec((r, KV_DIM)), pl.BlockSpec((r, LANES), lambda b: (0, 0)),
                  buf_spec, buf_spec, seq_spec((KV_DIM, wb)), seq_spec((KV_DIM, wb))],
        out_specs=[seq_spec((r, KV_DIM)), seq_spec((KV_DIM, wb)), seq_spec((KV_DIM, wb))],
        out_shape=[jax.ShapeDtypeStruct((db, r, KV_DIM), F32),
                   jax.ShapeDtypeStruct((db, KV_DIM, wb), F32),
                   jax.ShapeDtypeStruct((db, KV_DIM, wb), F32)],
        compiler_params=_cparams(("arbitrary",)),
        name="swa_sample",
    )(qe, sink_rows, buf_kt, buf_vt, knew, vnew)


_GELU_C0 = 0.7978845608028654
_GELU_C1 = 0.7978845608028654 * 0.044715


def _gelu_tanh(x):
    t = jnp.tanh(x * (_GELU_C0 + _GELU_C1 * (x * x)))
    return x * (0.5 + 0.5 * t)


def _ffn_kernel(*refs, halo, stride, multi_tile, rb, tf, resident):
    if resident:
        (o_ref, x_ref, gm_ref, sh_ref, sc_ref, gf_ref, gpm_ref, gpre_ref, gpost_ref, wo_ref,
         wup_ref, cw_ref, cb_ref, wd_ref, prev_ref, out_ref, tail_ref,
         x1_ref, h_ref, act_ref, buf0_ref, buf1_ref, carry_ref) = refs
    else:
        (o_ref, x_ref, gm_ref, sh_ref, sc_ref, gf_ref, gpm_ref, gpre_ref, gpost_ref, wo_ref,
         wg_ref, wv_ref, cwg_ref, cwv_ref, cbg_ref, cbv_ref, wd_ref, pg_ref, pv_ref, out_ref, tg_ref, tv_ref,
         x1_ref, h_ref, act_ref, buf0_ref, buf1_ref, carry_ref) = refs
    ti = pl.program_id(1)
    nc = act_ref.shape[0]
    dff = wd_ref.shape[0]
    tm = x_ref.shape[0]
    bufs = (buf0_ref, buf1_ref)

    def pair(gate_val):
        return jnp.concatenate(gate_val, axis=1)

    def chunk_cols(cc, full_ref, g_ref, v_ref):
        if resident:
            return pair([full_ref[:, tf * cc:tf * (cc + 1)], full_ref[:, dff + tf * cc:dff + tf * (cc + 1)]])
        return pair([g_ref[...], v_ref[...]])

    def project(cc, buf):
        prev = (chunk_cols(cc, prev_ref, None, None) if resident else chunk_cols(cc, None, pg_ref, pv_ref))
        if multi_tile:
            first = jnp.full((halo, 2 * tf), ti, jnp.int32) == 0
            prev = jnp.where(first, prev, carry_ref[cc])
        buf[0:halo, :] = prev
        w = chunk_cols(cc, wup_ref, None, None) if resident else chunk_cols(cc, None, wg_ref, wv_ref)
        buf[halo:halo + tm, :] = jnp.dot(h_ref[...], w, preferred_element_type=F32)
        last = buf[tm:tm + halo, :]
        if multi_tile:
            carry_ref[cc] = last
        if resident:
            tail_ref[:, tf * cc:tf * (cc + 1)] = last[:, :tf]
            tail_ref[:, dff + tf * cc:dff + tf * (cc + 1)] = last[:, tf:]
        else:
            tg_ref[...] = last[:, :tf]
            tv_ref[...] = last[:, tf:]

    def activate(cc, buf):
        cw = chunk_cols(cc, cw_ref, None, None) if resident else chunk_cols(cc, None, cwg_ref, cwv_ref)
        cb = chunk_cols(cc, cb_ref, None, None) if resident else chunk_cols(cc, None, cbg_ref, cbv_ref)
        for r0 in range(0, tm, rb):
            u0 = buf[halo + r0:halo + r0 + rb, :]
            u1 = buf[halo - stride + r0:halo - stride + r0 + rb, :]
            u2 = buf[halo - 2 * stride + r0:halo - 2 * stride + r0 + rb, :]
            conv = cb + cw[0:1] * u2 + cw[1:2] * u1 + cw[2:3] * u0
            act_ref[cc, r0:r0 + rb, :] = (_gelu_tanh(conv[:, :tf]) * conv[:, tf:]).astype(BF16)

    halves = [(0, tm)] if tm % 16 else [(0, tm // 2), (tm // 2, tm)]

    def mod_rows(ref, lo, hi):
        m = ref[...]
        return m if m.shape[0] == 1 else _rows(m, tm)[lo:hi]

    def prologue():
        ys = [jnp.dot(o_ref[lo:hi, :], wo_ref[...], preferred_element_type=F32) for lo, hi in halves]
        for (lo, hi), y in zip(halves, ys):
            x1 = x_ref[lo:hi, :] + mod_rows(gm_ref, lo, hi) * _rms(y, gpm_ref[...])
            x1_ref[lo:hi, :] = x1
            h_ref[lo:hi, :] = (_rms(x1, gpre_ref[...]) * (1.0 + mod_rows(sc_ref, lo, hi))
                               + mod_rows(sh_ref, lo, hi)).astype(BF16)

    def epilogue():
        kd = (nc - 1) * tf
        for lo, hi in halves:
            y = jnp.dot(act_ref[nc - 1, lo:hi, :], wd_ref[kd:, :], preferred_element_type=F32)
            if nc > 1:
                early = jnp.concatenate([act_ref[k, lo:hi, :] for k in range(nc - 1)], axis=1)
                y = y + jnp.dot(early, wd_ref[:kd, :], preferred_element_type=F32)
            out_ref[lo:hi, :] = x1_ref[lo:hi, :] + mod_rows(gf_ref, lo, hi) * _rms(y, gpost_ref[...])

    if resident:
        prologue()
        project(0, bufs[0])
        for cc in range(1, nc):
            activate(cc - 1, bufs[(cc - 1) % 2])
            project(cc, bufs[cc % 2])
        activate(nc - 1, bufs[(nc - 1) % 2])
        epilogue()
        return

    c = pl.program_id(2)

    @pl.when(c == 0)
    def _():
        prologue()
        project(c, bufs[0])

    for parity in range(2):
        @pl.when((c > 0) & (c < nc) & (lax.rem(c, 2) == parity))
        def _():
            activate(c - 1, bufs[1 - parity])
            project(c, bufs[parity])

    @pl.when(c == nc)
    def _():
        activate(c - 1, bufs[(nc - 1) % 2])
        epilogue()


def _ffn(o, x, gate_m, shift, scale, gate_f, gpost_m, gpre, gpost, wo, w_up, cw, cb, wd, prev,
         *, tm, tf, halo, stride, mod_map, resident):
    nb, r, d = x.shape
    dff = wd[0].shape[-2]
    nc = dff // tf
    nt = r // tm
    mr = shift[0].shape[-2]
    slabs = (gate_m, shift, scale, gate_f, gpost_m, gpre, gpost, wo, w_up, cw, cb, wd)
    gate_m_a, shift_a, scale_a, gate_f_a, gpost_m_a, gpre_a, gpost_a, wo_a, w_up_a, cw_a, cb_a, wd_a = (
        s[0] for s in slabs)
    multi_tile = nt > 1
    rb = 64 if tm % 64 == 0 else tm
    once = pl.Buffered(1)
    row_mode = dict(pipeline_mode=once) if nb * nt == 1 else {}
    kern = functools.partial(_ffn_kernel, halo=halo, stride=stride, multi_tile=multi_tile, rb=rb, tf=tf,
                             resident=resident)
    scratch = [
        pltpu.VMEM((tm, d), F32), pltpu.VMEM((tm, d), BF16), pltpu.VMEM((nc, tm, tf), BF16),
        pltpu.VMEM((halo + tm, 2 * tf), F32), pltpu.VMEM((halo + tm, 2 * tf), F32),
        pltpu.VMEM((nc, halo, 2 * tf) if multi_tile else (1, 8, LANES), F32),
    ]
    out_sds = jax.ShapeDtypeStruct((nb, r, d), F32)
    if resident:
        im = lambda b, t: (b, t, 0)
        seq = lambda b, t: (b, 0, 0)
        const = lambda b, t: (0, 0)
        mod_spec = lambda p: _slab(p, (None, mr, d), lambda b, t: (mod_map(b), 0, 0))
        vec_spec = lambda p: _slab(p, (1, d), const)
        whole = lambda p, **kw: _slab(p, p[0].shape[-2:], const, **kw)
        out, tail = pl.pallas_call(
            kern, grid=(nb, nt),
            in_specs=[
                pl.BlockSpec((None, tm, Q_DIM), im, **row_mode), pl.BlockSpec((None, tm, d), im, **row_mode),
                mod_spec(gate_m), mod_spec(shift), mod_spec(scale), mod_spec(gate_f),
                vec_spec(gpost_m), vec_spec(gpre), vec_spec(gpost),
                whole(wo, pipeline_mode=once), whole(w_up, pipeline_mode=once), whole(cw), whole(cb),
                whole(wd, pipeline_mode=once),
                pl.BlockSpec((None, halo, 2 * dff), seq),
            ],
            out_specs=[pl.BlockSpec((None, tm, d), im), pl.BlockSpec((None, halo, 2 * dff), seq)],
            out_shape=[out_sds, jax.ShapeDtypeStruct((nb, halo, 2 * dff), F32)],
            scratch_shapes=scratch,
            compiler_params=_cparams(("arbitrary", "arbitrary")),
            name="mixer_out_conv_ffn",
        )(o, x, gate_m_a, shift_a, scale_a, gate_f_a, gpost_m_a, gpre_a, gpost_a, wo_a, w_up_a, cw_a, cb_a,
          wd_a, prev)
        return out, tail

    assert not multi_tile
    last = nc - 1
    im = lambda b, t, c: (b, t, 0)
    const = lambda b, t, c: (0, 0)
    mod_spec = lambda p: _slab(p, (None, mr, d), lambda b, t, c: (mod_map(b), 0, 0))
    vec_spec = lambda p: _slab(p, (1, d), const)
    whole = lambda p, **kw: _slab(p, p[0].shape[-2:], const, **kw)
    proj = lambda p, off: _slab(p, (p[0].shape[-2], tf), lambda b, t, c: (0, off + jnp.minimum(c, last)))
    actv = lambda p, off: _slab(p, (p[0].shape[-2], tf), lambda b, t, c: (0, off + jnp.maximum(c - 1, 0)))
    halo_spec = lambda off: pl.BlockSpec((None, halo, tf), lambda b, t, c: (b, 0, off + jnp.minimum(c, last)))
    tail_sds = jax.ShapeDtypeStruct((nb, halo, dff), F32)
    out, tail_g, tail_v = pl.pallas_call(
        kern, grid=(nb, nt, nc + 1),
        in_specs=[
            pl.BlockSpec((None, tm, Q_DIM), im, **row_mode), pl.BlockSpec((None, tm, d), im, **row_mode),
            mod_spec(gate_m), mod_spec(shift), mod_spec(scale), mod_spec(gate_f),
            vec_spec(gpost_m), vec_spec(gpre), vec_spec(gpost),
            whole(wo, pipeline_mode=once),
            proj(w_up, 0), proj(w_up, nc), actv(cw, 0), actv(cw, nc), actv(cb, 0), actv(cb, nc),
            whole(wd, pipeline_mode=once),
            halo_spec(0), halo_spec(nc),
        ],
        out_specs=[pl.BlockSpec((None, tm, d), im), halo_spec(0), halo_spec(0)],
        out_shape=[out_sds, tail_sds, tail_sds],
        scratch_shapes=scratch,
        compiler_params=_cparams(("arbitrary", "arbitrary", "arbitrary")),
        name="mixer_out_conv_ffn_chunked",
    )(o, x, gate_m_a, shift_a, scale_a, gate_f_a, gpost_m_a, gpre_a, gpost_a, wo_a, w_up_a, w_up_a,
      cw_a, cw_a, cb_a, cb_a, wd_a, prev, prev)
    return out, jnp.concatenate([tail_g, tail_v], axis=-1)


def _rope_tables(pos):
    inv_freq = jnp.power(jnp.float32(ROPE_THETA), -jnp.arange(ROT_HALF, dtype=F32) * (2.0 / ROT_DIM))
    ang = pos.astype(F32)[:, None] * inv_freq[None, :]
    cos, sin = jnp.cos(ang), jnp.sin(ang)
    n = pos.shape[0]
    ones = jnp.ones((n, HEAD_DIM - ROT_DIM), F32)
    zeros_r = jnp.zeros((n, HEAD_DIM - ROT_DIM), F32)
    zeros_h = jnp.zeros((n, ROT_HALF), F32)
    cq = jnp.concatenate([cos, cos, ones], axis=1)
    s1 = jnp.concatenate([zeros_h, sin, zeros_r], axis=1)
    s2 = jnp.concatenate([-sin, zeros_h, zeros_r], axis=1)
    rep = LANES // HEAD_DIM
    return (jnp.tile(cq, (1, rep)), jnp.tile(s1, (1, rep)), jnp.tile(s2, (1, rep)), cos.T, sin.T)


def _kv_onehot():
    oh = np.zeros((N_QBLK, 2, N_KV_HEADS), np.float32)
    for i in range(N_QBLK):
        for p in range(2):
            oh[i, p, 2 * (i // GROUP) + p] = 1.0
    return oh


def _expand_q(q_tm):
    dt, db, _ = q_tm.shape
    q5 = q_tm.reshape(dt, db, N_QBLK, 2, HEAD_DIM).transpose(1, 2, 3, 0, 4)
    oh = jnp.asarray(_kv_onehot(), q_tm.dtype)
    q6 = q5[:, :, :, :, None, :] * oh[None, :, :, None, :, None]
    return q6.reshape(db, N_QBLK * 2 * dt, KV_DIM)


def _select_o(o_full, dt):
    db = o_full.shape[0]
    o6 = o_full.reshape(db, N_QBLK, 2, dt, N_KV_HEADS, HEAD_DIM)
    o5 = jnp.sum(o6 * jnp.asarray(_kv_onehot())[None, :, :, None, :, None], axis=4)
    return o5.transpose(3, 0, 1, 2, 4).reshape(dt, db, Q_DIM).astype(BF16)


def _rows_of_heads(v, dt):
    vp = v[jnp.asarray(HEAD_PERM)]
    return jnp.broadcast_to(vp[:, None, None], (N_HEADS, dt, LANES)).reshape(N_HEADS * dt, LANES).astype(F32)


def _from_t(kt, lead):
    n = kt.shape[-1]
    x = kt.reshape(lead + (N_KV_HEADS, HEAD_DIM, n))
    nd = len(lead)
    return x.transpose(tuple(range(nd)) + (nd + 2, nd, nd + 1))


def kernel(x_prompt, x_sample, cache_sb_k, cache_sb_v, state_swa_k, state_swa_v, state_conv, page_table,
           c_prompt, c_sample, mod_w, mod_b, norm_mix_pre, norm_mix_post, norm_ffn_pre, norm_ffn_post,
           sb_w_qkv, sb_w_o, sb_bias, swa_w_qkv, swa_w_o, swa_sinks, ffn_w_up, ffn_conv_w, ffn_conv_b,
           ffn_w_down):
    b, t, d = x_prompt.shape
    db, dt, _ = x_sample.shape
    depth = mod_w.shape[0]
    n_pages = page_table.shape[1]
    past_len = n_pages * PAGE_SIZE
    wb = state_swa_k.shape[2]
    dff = ffn_w_down.shape[1]
    n_phys = cache_sb_k.shape[1]

    tm_p = min(512, t)
    tm_f = min(512, t)
    tf = 256

    mod = _modulation(jnp.concatenate([c_prompt, c_sample], axis=0), mod_w, mod_b)
    mod_p = mod[:, :, :b].reshape(depth, N_MOD, b, 1, d)
    mod_s = mod[:, :, b:].reshape(depth, N_MOD, 1, db, d)

    cache_kt = cache_sb_k.transpose(0, 1, 3, 4, 2).reshape(-1, n_phys, KV_DIM, PAGE_SIZE)
    cache_vt = cache_sb_v.transpose(0, 1, 3, 4, 2).reshape(-1, n_phys, KV_DIM, PAGE_SIZE)
    swa_kt = state_swa_k.transpose(0, 1, 3, 4, 2).reshape(-1, db, KV_DIM, wb)
    swa_vt = state_swa_v.transpose(0, 1, 3, 4, 2).reshape(-1, db, KV_DIM, wb)

    qcols = jnp.asarray(np.concatenate([np.arange(HEAD_DIM) + HEAD_DIM * h for h in HEAD_PERM]))
    perm = jnp.asarray(HEAD_PERM)

    def prep_mixer(w_qkv, w_o):
        return (w_qkv[:, :, :Q_DIM][:, :, qcols].astype(BF16),
                w_qkv[:, :, Q_DIM:].transpose(0, 2, 1).astype(BF16),
                w_o[:, qcols, :].astype(BF16))
    mixer_w = (prep_mixer(sb_w_qkv, sb_w_o), prep_mixer(swa_w_qkv, swa_w_o))
    w_up_all = ffn_w_up.astype(BF16)
    wd_all = ffn_w_down.astype(BF16)
    cb_all = ffn_conv_b.reshape(depth, 1, 2 * dff)
    gains = [g.reshape(depth, 1, d) for g in (norm_mix_pre, norm_mix_post, norm_ffn_pre, norm_ffn_post)]

    rope_p = _rope_tables(jnp.arange(t))
    rope_p = tuple(a[None] for a in rope_p) + (False,)
    cq, s1, s2, ck, sk = _rope_tables(past_len + jnp.arange(dt))
    rope_s = (cq[:, None, :], s1[:, None, :], s2[:, None, :],
              jnp.broadcast_to(ck.T[:, :, None], (dt, ROT_HALF, db)),
              jnp.broadcast_to(sk.T[:, :, None], (dt, ROT_HALF, db)), True)

    xp = x_prompt
    xs = x_sample.transpose(1, 0, 2)
    pmap = lambda bi: bi
    smap = lambda bi: 0

    outs = {k: [] for k in ("sbkp", "sbvp", "sbks", "sbvs", "swkp", "swvp", "swks", "swvs", "cp", "cs")}
    for i in range(depth):
        j = i // 2
        is_sb = i % 2 == 0
        wq, wkvt, wo = ((w, (j,)) for w in mixer_w[0 if is_sb else 1])
        g_pre, g_post, gf_pre, gf_post = ((g, (i,)) for g in gains)
        mp = [(mod_p, (i, k)) for k in range(N_MOD)]
        ms = [(mod_s, (i, k)) for k in range(N_MOD)]

        qp, ktp, vtp, kvbp = _qkv(xp, mp[0], mp[1], g_pre, wq, wkvt, None if is_sb else rope_p,
                                  tm=tm_p, mod_map=pmap)
        qs, kts, vts, _ = _qkv(xs, ms[0], ms[1], g_pre, wq, wkvt, None if is_sb else rope_s,
                               tm=db, mod_map=smap)
        qe = _expand_q(qs)
        kn = kts.transpose(2, 1, 0)
        vn = vts.transpose(2, 1, 0)
        if is_sb:
            op = _sb_prompt(qp, kvbp, sb_bias[j][perm])
            pad = ((0, 0), (0, 0), (0, LANES - dt))
            o_full = _sb_sample(page_table, qe, _rows_of_heads(sb_bias[j], dt),
                                jnp.pad(kn, pad).astype(BF16), jnp.pad(vn, pad).astype(BF16),
                                cache_kt, cache_vt, j, dt)
            outs["sbkp"].append(_from_t(ktp, (b,)))
            outs["sbvp"].append(_from_t(vtp, (b,)))
            outs["sbks"].append(_from_t(kn, (db,)))
            outs["sbvs"].append(_from_t(vn, (db,)))
        else:
            op = _swa_prompt(qp, kvbp, swa_sinks[j][perm])
            pad = ((0, 0), (0, 0), (wb - dt, 0))
            o_full, ko, vo = _swa_sample(qe, _rows_of_heads(swa_sinks[j], dt), swa_kt, swa_vt, j,
                                         jnp.pad(kn, pad), jnp.pad(vn, pad), dt, past_len)
            keep = min(WINDOW, t)
            outs["swkp"].append(_from_t(ktp[:, :, t - keep:], (b,)))
            outs["swvp"].append(_from_t(vtp[:, :, t - keep:], (b,)))
            outs["swks"].append(_from_t(ko, (db,)))
            outs["swvs"].append(_from_t(vo, (db,)))
        os_ = _select_o(o_full, dt)

        w_up, wd, cw, cb = (w_up_all, (i,)), (wd_all, (i,)), (ffn_conv_w, (i,)), (cb_all, (i,))

        halo_p = 8
        zero_prev = jnp.zeros((b, halo_p, 2 * dff), F32)
        xp, tail_p = _ffn(op, xp, mp[2], mp[3], mp[4], mp[5], g_post, gf_pre, gf_post, wo,
                          w_up, cw, cb, wd, zero_prev,
                          tm=tm_f, tf=tf, halo=halo_p, stride=1, mod_map=pmap, resident=True)
        st = state_conv[i].transpose(1, 0, 2).reshape(1, 2 * db, 2 * dff)
        xs1, tail_s = _ffn(os_.reshape(1, dt * db, Q_DIM), xs.reshape(1, dt * db, d),
                           ms[2], ms[3], ms[4], ms[5], g_post, gf_pre, gf_post, wo,
                           w_up, cw, cb, wd, st,
                           tm=dt * db, tf=tf, halo=2 * db, stride=db, mod_map=smap, resident=False)
        xs = xs1.reshape(dt, db, d)
        outs["cp"].append(tail_p[:, halo_p - 2:])
        outs["cs"].append(tail_s.reshape(2, db, 2 * dff).transpose(1, 0, 2))

    st_ = lambda k: jnp.stack(outs[k])
    return (xp, xs.transpose(1, 0, 2),
            st_("sbkp"), st_("sbvp"), st_("sbks"), st_("sbvs"),
            st_("swkp"), st_("swvp"), st_("swks"), st_("swvs"),
            st_("cp"), st_("cs"))
```

```python
import functools

import numpy as np
import jax
import jax.numpy as jnp
from jax import lax
from jax.experimental import pallas as pl
from jax.experimental.pallas import tpu as pltpu

F32 = jnp.float32
BF16 = jnp.bfloat16

HEAD_DIM = 64
N_HEADS = 16
N_KV_HEADS = 4
GROUP = N_HEADS // N_KV_HEADS
Q_DIM = N_HEADS * HEAD_DIM
KV_DIM = N_KV_HEADS * HEAD_DIM
PAGE_SIZE = 128
WINDOW = 128
Q_BLOCK = 128
ROT_DIM = HEAD_DIM // 4
ROT_HALF = ROT_DIM // 2
ROPE_THETA = 500000.0
RMS_EPS = 1e-6
N_MOD = 6
LANES = 128
N_QBLK = Q_DIM // LANES
N_PAIR = KV_DIM // LANES
VMEM_LIMIT = 56 * 1024 * 1024

HEAD_PERM = [4 * (2 * (i // 4) + p) + (i % 4) for i in range(N_QBLK) for p in range(2)]


def _cparams(sem):
    return pltpu.CompilerParams(dimension_semantics=sem, vmem_limit_bytes=VMEM_LIMIT)


def _rms(x, g):
    return x * lax.rsqrt(jnp.mean(x * x, axis=-1, keepdims=True) + RMS_EPS) * g


def _rows(m, rows):
    mr = m.shape[0]
    if mr == 1 or mr == rows:
        return m
    return jnp.concatenate([m] * (rows // mr), axis=0)


def _norm_mod(x, g, shift, scale):
    rows = x.shape[0]
    return _rms(x, g) * (1.0 + _rows(scale, rows)) + _rows(shift, rows)


def _slab(param, tail_block, tail_index, **kw):
    _, lead = param
    lead = tuple(lead)
    return pl.BlockSpec((None,) * len(lead) + tuple(tail_block),
                        lambda *grid: lead + tuple(tail_index(*grid)), **kw)


def _mod_kernel(c_ref, w_ref, b_ref, o_ref):
    c = c_ref[...]
    s = c / (1.0 + jnp.exp(-c))
    o_ref[...] = jnp.dot(s.astype(BF16), w_ref[...].astype(BF16), preferred_element_type=F32) + b_ref[...]


def _modulation(c_all, mod_w, mod_b):
    depth, d, _ = mod_w.shape
    n = c_all.shape[0]
    return pl.pallas_call(
        _mod_kernel,
        grid=(depth, N_MOD),
        in_specs=[
            pl.BlockSpec((n, d), lambda i, k: (0, 0)),
            pl.BlockSpec((None, d, d), lambda i, k: (i, 0, k)),
            pl.BlockSpec((None, None, 1, d), lambda i, k: (i, k, 0, 0)),
        ],
        out_specs=pl.BlockSpec((None, None, n, d), lambda i, k: (i, k, 0, 0)),
        out_shape=jax.ShapeDtypeStruct((depth, N_MOD, n, d), F32),
        compiler_params=_cparams(("arbitrary", "arbitrary")),
        name="modulation",
    )(c_all, mod_w, mod_b.reshape(depth, N_MOD, 1, d))


def _qkv_kernel(*refs, rope, kc):
    if rope:
        (x_ref, sh_ref, sc_ref, g_ref, wq_ref, wkv_ref, cq_ref, s1_ref, s2_ref, ck_ref, sk_ref,
         q_ref, kt_ref, vt_ref, kvb_ref) = refs
    else:
        x_ref, sh_ref, sc_ref, g_ref, wq_ref, wkv_ref, q_ref, kt_ref, vt_ref, kvb_ref = refs
    tm = x_ref.shape[0]
    h = _norm_mod(x_ref[...], g_ref[...], sh_ref[...], sc_ref[...]).astype(BF16)
    q = jnp.dot(h, wq_ref[...], preferred_element_type=F32)
    kvt = lax.dot_general(wkv_ref[...], h, (((1,), (1,)), ((), ())),
                          preferred_element_type=F32)
    scale = HEAD_DIM ** -0.5
    if rope:
        cq, s1, s2 = cq_ref[...], s1_ref[...], s2_ref[...]
        for i in range(N_QBLK):
            xr = q[:, LANES * i:LANES * (i + 1)]
            rot = xr * cq + pltpu.roll(xr, ROT_HALF, 1) * s1 + pltpu.roll(xr, LANES - ROT_HALF, 1) * s2
            q_ref[:, LANES * i:LANES * (i + 1)] = (rot * scale).astype(BF16)
        ck, sk = ck_ref[...], sk_ref[...]
        pieces = []
        for c in range(N_KV_HEADS):
            base = HEAD_DIM * c
            x1 = kvt[base:base + ROT_HALF]
            x2 = kvt[base + ROT_HALF:base + ROT_DIM]
            pieces += [x1 * ck - x2 * sk, x2 * ck + x1 * sk, kvt[base + ROT_DIM:base + HEAD_DIM]]
        kt = jnp.concatenate(pieces, axis=0)
    else:
        q_ref[...] = (q * scale).astype(BF16)
        kt = kvt[:KV_DIM]
    vt = kvt[KV_DIM:]
    kt_ref[...] = kt
    vt_ref[...] = vt
    kb = kt.astype(BF16)
    vb = vt.astype(BF16)
    for c in range(tm // kc):
        kvb_ref[c, :KV_DIM, :] = kb[:, kc * c:kc * (c + 1)]
        kvb_ref[c, KV_DIM:, :] = vb[:, kc * c:kc * (c + 1)]


def _qkv(x, shift, scale, g, wq, wkvt, rope_tabs, *, tm, mod_map):
    nb, r, d = x.shape
    kc = min(LANES, tm)
    nt = r // tm
    rope = rope_tabs is not None
    mr = shift[0].shape[-2]
    const = lambda b, t: (0, 0)
    mod_index = lambda b, t: (mod_map(b), 0, 0)
    in_specs = [
        pl.BlockSpec((None, tm, d), lambda b, t: (b, t, 0)),
        _slab(shift, (None, mr, d), mod_index),
        _slab(scale, (None, mr, d), mod_index),
        _slab(g, (1, d), const),
        _slab(wq, wq[0].shape[-2:], const),
        _slab(wkvt, wkvt[0].shape[-2:], const),
    ]
    args = [x, shift[0], scale[0], g[0], wq[0], wkvt[0]]
    if rope:
        cq, s1, s2, ck, sk, per_b = rope_tabs
        if per_b:
            qspec = pl.BlockSpec((None, 1, LANES), lambda b, t: (b, 0, 0))
            kspec = pl.BlockSpec((None, ROT_HALF, tm), lambda b, t: (b, 0, 0))
        else:
            qspec = pl.BlockSpec((None, tm, LANES), lambda b, t: (0, t, 0))
            kspec = pl.BlockSpec((None, ROT_HALF, tm), lambda b, t: (0, 0, t))
        in_specs += [qspec, qspec, qspec, kspec, kspec]
        args += [cq, s1, s2, ck, sk]
    out_shape = [
        jax.ShapeDtypeStruct((nb, r, Q_DIM), BF16),
        jax.ShapeDtypeStruct((nb, KV_DIM, r), F32),
        jax.ShapeDtypeStruct((nb, KV_DIM, r), F32),
        jax.ShapeDtypeStruct((nb, r // kc, 2 * KV_DIM, kc), BF16),
    ]
    out_specs = [
        pl.BlockSpec((None, tm, Q_DIM), lambda b, t: (b, t, 0)),
        pl.BlockSpec((None, KV_DIM, tm), lambda b, t: (b, 0, t)),
        pl.BlockSpec((None, KV_DIM, tm), lambda b, t: (b, 0, t)),
        pl.BlockSpec((None, tm // kc, 2 * KV_DIM, kc), lambda b, t: (b, t, 0, 0)),
    ]
    return pl.pallas_call(
        functools.partial(_qkv_kernel, rope=rope, kc=kc),
        grid=(nb, nt),
        in_specs=in_specs,
        out_specs=out_specs,
        out_shape=out_shape,
        compiler_params=_cparams(("arbitrary", "arbitrary")),
        name="qkv_rope" if rope else "qkv",
    )(*args)


LOG2E = 1.4426950408889634


def _sb_split(z, mask):
    n = z.shape[1] // LANES
    sp = jnp.maximum(z, 0.0) + jnp.log(1.0 + jnp.exp2(jnp.abs(z) * (-LOG2E)))
    if mask is not None:
        sp = jnp.where(mask, sp, 0.0)
    hi_f = lax.bitcast_convert_type(
        lax.bitcast_convert_type(sp, jnp.uint32) & jnp.uint32(0xFFFF0000), F32)
    hi = hi_f.astype(BF16)
    lo = (sp - hi_f).astype(BF16)
    return jnp.concatenate(
        [jnp.concatenate([hi[:, LANES * k:LANES * (k + 1)], lo[:, LANES * k:LANES * (k + 1)]], axis=1)
         for k in range(n)], axis=0)


def _sb_log_weights(z, t, carry):
    r = z.shape[0]
    n = z.shape[1] // LANES
    tails = [None] * n
    for k in reversed(range(n)):
        blk = t[r * k:r * (k + 1)]
        tails[k] = blk[:, :LANES] + carry
        carry = carry + blk[:, LANES:]
    tail = tails[0] if n == 1 else jnp.concatenate(tails, axis=1)
    return z + tail, carry


def _sb_finish(z, t, mask, carry):
    lw, carry = _sb_log_weights(z, t, carry)
    a = jnp.exp(lw)
    if mask is not None:
        a = jnp.where(mask, a, 0.0)
    return a.astype(BF16), carry


def _cumsum_total_matrix():
    j = np.arange(LANES)[:, None]
    c = np.arange(LANES)[None, :]
    half = np.concatenate([-(j >= c).astype(np.float32), -np.ones((LANES, LANES), np.float32)], axis=1)
    return jnp.asarray(np.concatenate([half, half], axis=0), dtype=BF16)


def _sb_prompt_kernel(bias_ref, q_ref, kvb_ref, w_ref, o_ref, *scratch):
    qi = pl.program_id(1)
    rows = GROUP * Q_BLOCK
    lane = lax.broadcasted_iota(jnp.int32, (Q_BLOCK, LANES), 1)
    top = qi // 2
    qs_refs, c_refs, acc_refs = scratch[0:4], scratch[4:8], scratch[8:12]
    chains = [(j, p) for j in range(N_PAIR) for p in range(2)]
    nt_dims = (((1,), (1,)), ((), ()))
    for n, (j, p) in enumerate(chains):
        in_half = (lane // HEAD_DIM) == p
        for g in range(GROUP):
            blk = q_ref[:, LANES * (GROUP * j + g):LANES * (GROUP * j + g + 1)]
            qs_refs[n][Q_BLOCK * g:Q_BLOCK * (g + 1), :] = jnp.where(in_half, blk, 0)
        c_refs[n][...] = jnp.zeros_like(c_refs[n])
        acc_refs[n][...] = jnp.zeros_like(acc_refs[n])

    def chunk(ci, masked):
        kts = [jnp.concatenate([kvb_ref[2 * ci, LANES * j:LANES * (j + 1), :],
                                kvb_ref[2 * ci + 1, LANES * j:LANES * (j + 1), :]], axis=1)
               for j in range(N_PAIR)]
        vts = [jnp.concatenate([kvb_ref[2 * ci, KV_DIM + LANES * j:KV_DIM + LANES * (j + 1), :],
                                kvb_ref[2 * ci + 1, KV_DIM + LANES * j:KV_DIM + LANES * (j + 1), :]], axis=1)
               for j in range(N_PAIR)]
        mask = None
        if masked:
            r_idx = lax.broadcasted_iota(jnp.int32, (rows, 2 * LANES), 0) & (Q_BLOCK - 1)
            c_idx = lax.broadcasted_iota(jnp.int32, (rows, 2 * LANES), 1)
            mask = (ci * 2 * LANES + c_idx) < (qi * Q_BLOCK + r_idx)
        zs = [jnp.dot(qs_refs[n][...], kts[j], preferred_element_type=F32) for n, (j, p) in enumerate(chains)]
        zs = [jnp.concatenate(
            [zs[n][Q_BLOCK * g:Q_BLOCK * (g + 1)] + bias_ref[2 * (GROUP * j + g) + p]
             for g in range(GROUP)], axis=0) for n, (j, p) in enumerate(chains)]
        hls = [_sb_split(z, mask) for z in zs]
        ts = [jnp.dot(hl, w_ref[...], preferred_element_type=F32) for hl in hls]
        aw = [_sb_finish(zs[n], ts[n], mask, c_refs[n][...]) for n in range(len(chains))]
        for n in range(len(chains)):
            c_refs[n][...] = aw[n][1]
        pv = [lax.dot_general(aw[n][0], vts[j], nt_dims, preferred_element_type=F32)
              for n, (j, p) in enumerate(chains)]
        for n in range(len(chains)):
            acc_refs[n][...] += pv[n]

    chunk(top, True)

    def body(n, _):
        chunk(top - 1 - n, False)
        return 0

    lax.fori_loop(0, top, body, 0)
    lane_r = lax.broadcasted_iota(jnp.int32, (rows, LANES), 1)
    for j in range(N_PAIR):
        o_pair = jnp.where((lane_r // HEAD_DIM) == 0, acc_refs[2 * j][...],
                           acc_refs[2 * j + 1][...]).astype(BF16)
        for g in range(GROUP):
            o_ref[:, LANES * (GROUP * j + g):LANES * (GROUP * j + g + 1)] = \
                o_pair[Q_BLOCK * g:Q_BLOCK * (g + 1)]


def _sb_prompt(q, kvb, bias_perm):
    b, t, _ = q.shape
    nq = t // Q_BLOCK
    nck = kvb.shape[1]
    rows = GROUP * Q_BLOCK
    return pl.pallas_call(
        _sb_prompt_kernel,
        grid_spec=pltpu.PrefetchScalarGridSpec(
            num_scalar_prefetch=1,
            grid=(b, nq),
            in_specs=[
                pl.BlockSpec((None, Q_BLOCK, Q_DIM), lambda bi, qi, s: (bi, qi, 0)),
                pl.BlockSpec((None, nck, 2 * KV_DIM, LANES), lambda bi, qi, s: (bi, 0, 0, 0)),
                pl.BlockSpec((2 * LANES, 2 * LANES), lambda bi, qi, s: (0, 0)),
            ],
            out_specs=pl.BlockSpec((None, Q_BLOCK, Q_DIM), lambda bi, qi, s: (bi, qi, 0)),
            scratch_shapes=([pltpu.VMEM((rows, LANES), BF16)] * N_KV_HEADS
                            + [pltpu.VMEM((rows, LANES), F32)] * (2 * N_KV_HEADS)),
        ),
        out_shape=jax.ShapeDtypeStruct((b, t, Q_DIM), BF16),
        compiler_params=_cparams(("arbitrary", "arbitrary")),
        name="sb_prompt",
    )(bias_perm, q, kvb, _cumsum_total_matrix())


def _sb_sample_kernel(*refs, n_pp, dt, steps, n_seq, layer):
    (sched_ref, qe_ref, bias_ref, w_ref, kn_ref, vn_ref, ck_ref, cv_ref,
     o_ref, c_ref, acc_ref, lw0_ref, lw1_ref, kbuf_ref, vbuf_ref, ksem, vsem) = refs
    lws = (lw0_ref, lw1_ref)
    g = pl.program_id(0)
    last = n_seq * steps
    seq_a = g // steps
    seq_b = (g - 1) // steps
    r = qe_ref.shape[0]
    nt_dims = (((1,), (1,)), ((), ()))
    halves = 2 if n_pp % 2 == 0 else 1

    def page_copy(cache_ref, buf_ref, sem, page, slot, k):
        return pltpu.make_async_copy(cache_ref.at[layer, page], buf_ref.at[slot, k], sem.at[slot])

    def start_pages(cache_ref, buf_ref, sem, group, slot):
        for k in range(n_pp):
            page_copy(cache_ref, buf_ref, sem, sched_ref[group * n_pp + k], slot, k).start()

    def wait_pages(cache_ref, buf_ref, sem, slot):
        for k in range(n_pp):
            page_copy(cache_ref, buf_ref, sem, 0, slot, k).wait()

    def logits(slot):
        qe, bias = qe_ref[...], bias_ref[...]
        return jnp.concatenate(
            [jnp.dot(qe, kbuf_ref[slot, k].astype(BF16), preferred_element_type=F32) + bias
             for k in range(n_pp)], axis=1)

    def log_weights(z, lw_ref):
        hl = _sb_split(z, None)
        hr = hl.shape[0] // halves
        t = jnp.concatenate([jnp.dot(hl[i:i + hr], w_ref[...], preferred_element_type=F32)
                             for i in range(0, hl.shape[0], hr)], axis=0)
        lw, c_new = _sb_log_weights(z, t, c_ref[...])
        c_ref[...] = c_new
        lw_ref[...] = lw

    def weights_v(a, slot):
        hk = n_pp // halves
        pv = None
        for k0 in range(0, n_pp, hk):
            vt = jnp.concatenate([vbuf_ref[slot, k].astype(BF16) for k in range(k0, k0 + hk)], axis=1)
            part = lax.dot_general(a[:, PAGE_SIZE * k0:PAGE_SIZE * (k0 + hk)], vt, nt_dims,
                                   preferred_element_type=F32)
            pv = part if pv is None else pv + part
        return pv

    def accumulate(pv):
        slot = lax.rem(seq_b, 2)
        acc = acc_ref[slot] + pv
        acc_ref[slot] = acc
        o_ref[...] = acc

    @pl.when((lax.rem(g, steps) == 0) & (g < last))
    def _():
        t_row = lax.broadcasted_iota(jnp.int32, (r, LANES), 0) % dt
        t_key = lax.broadcasted_iota(jnp.int32, (r, LANES), 1)
        z = jnp.dot(qe_ref[...], kn_ref[...], preferred_element_type=F32) + bias_ref[...]
        mask = t_key < t_row
        t = jnp.dot(_sb_split(z, mask), w_ref[...], preferred_element_type=F32)
        a, c_new = _sb_finish(z, t, mask, jnp.zeros((r, LANES), F32))
        c_ref[...] = c_new
        acc_ref[lax.rem(seq_a, 2)] = lax.dot_general(a, vn_ref[...], nt_dims, preferred_element_type=F32)

    @pl.when(g == 0)
    def _():
        start_pages(ck_ref, kbuf_ref, ksem, 0, 0)
        start_pages(ck_ref, kbuf_ref, ksem, min(1, last - 1), 1)
        start_pages(cv_ref, vbuf_ref, vsem, 0, 0)
        wait_pages(ck_ref, kbuf_ref, ksem, 0)
        log_weights(logits(0), lws[0])

    for parity in range(2):
        @pl.when((g > 0) & (g < last) & (lax.rem(g, 2) == parity))
        def _():
            start_pages(ck_ref, kbuf_ref, ksem, jnp.minimum(g + 1, last - 1), 1 - parity)
            start_pages(cv_ref, vbuf_ref, vsem, g, parity)
            wait_pages(ck_ref, kbuf_ref, ksem, parity)
            wait_pages(cv_ref, vbuf_ref, vsem, 1 - parity)
            a = jnp.exp(lws[1 - parity][...]).astype(BF16)
            z = logits(parity)
            pv = weights_v(a, 1 - parity)
            log_weights(z, lws[parity])
            accumulate(pv)

    @pl.when(g == last)
    def _():
        wait_pages(ck_ref, kbuf_ref, ksem, last % 2)
        wait_pages(cv_ref, vbuf_ref, vsem, (last - 1) % 2)
        accumulate(weights_v(jnp.exp(lws[(last - 1) % 2][...]).astype(BF16), (last - 1) % 2))


def _sb_sample(page_table, qe, bias_rows, knew, vnew, cache_kt, cache_vt, layer, dt):
    db, r, _ = qe.shape
    n_pages = page_table.shape[1]
    n_pp = min(16, n_pages)
    while n_pages % n_pp:
        n_pp -= 1
    steps = n_pages // n_pp
    last = db * steps
    sched = page_table.reshape(db, steps, n_pp)[:, ::-1, :].reshape(last * n_pp)

    seq_a = lambda g, sc: (jnp.minimum(g // steps, db - 1), 0, 0)
    in_specs = [
        pl.BlockSpec((None, r, KV_DIM), seq_a),
        pl.BlockSpec((r, LANES), lambda g, sc: (0, 0)),
        pl.BlockSpec((2 * LANES, 2 * LANES), lambda g, sc: (0, 0)),
        pl.BlockSpec((None, KV_DIM, LANES), seq_a),
        pl.BlockSpec((None, KV_DIM, LANES), seq_a),
        pl.BlockSpec(memory_space=pl.ANY),
        pl.BlockSpec(memory_space=pl.ANY),
    ]
    page_buf = pltpu.VMEM((2, n_pp, KV_DIM, PAGE_SIZE), F32)
    return pl.pallas_call(
        functools.partial(_sb_sample_kernel, n_pp=n_pp, dt=dt, steps=steps, n_seq=db, layer=layer),
        grid_spec=pltpu.PrefetchScalarGridSpec(
            num_scalar_prefetch=1,
            grid=(last + 1,),
            in_specs=in_specs,
            out_specs=pl.BlockSpec((None, r, KV_DIM), lambda g, sc: (jnp.maximum(g - 1, 0) // steps, 0, 0)),
            scratch_shapes=[pltpu.VMEM((r, LANES), F32), pltpu.VMEM((2, r, KV_DIM), F32),
                            pltpu.VMEM((r, n_pp * PAGE_SIZE), F32), pltpu.VMEM((r, n_pp * PAGE_SIZE), F32),
                            page_buf, page_buf,
                            pltpu.SemaphoreType.DMA((2,)), pltpu.SemaphoreType.DMA((2,))],
        ),
        out_shape=jax.ShapeDtypeStruct((db, r, KV_DIM), F32),
        compiler_params=_cparams(("arbitrary",)),
        name="sb_sample",
    )(sched, qe, bias_rows, _cumsum_total_matrix(), knew, vnew, cache_kt, cache_vt)


def _swa_prompt_kernel(sink_ref, q_ref, kvb_ref, o_ref, half_ref):
    qi = pl.program_id(1)
    rows = GROUP * Q_BLOCK
    lane = lax.broadcasted_iota(jnp.int32, (Q_BLOCK, LANES), 1)
    r_idx = lax.broadcasted_iota(jnp.int32, (rows, 2 * LANES), 0) & (Q_BLOCK - 1)
    c_idx = lax.broadcasted_iota(jnp.int32, (rows, 2 * LANES), 1)
    first_col = jnp.where(qi > 0, 0, WINDOW)
    mask = (c_idx >= r_idx) & (c_idx <= r_idx + WINDOW) & (c_idx >= first_col)
    prev = jnp.maximum(qi - 1, 0)
    ones = jnp.ones((LANES, 2 * LANES), BF16)
    for j in range(N_PAIR):
        kt = jnp.concatenate([kvb_ref[prev, LANES * j:LANES * (j + 1), :],
                              kvb_ref[qi, LANES * j:LANES * (j + 1), :]], axis=1)
        vt = jnp.concatenate([kvb_ref[prev, KV_DIM + LANES * j:KV_DIM + LANES * (j + 1), :],
                              kvb_ref[qi, KV_DIM + LANES * j:KV_DIM + LANES * (j + 1), :]], axis=1)
        vt_ones = jnp.concatenate([vt, ones], axis=0)
        for p in range(2):
            in_half = (lane // HEAD_DIM) == p
            qs = jnp.concatenate(
                [jnp.where(in_half, q_ref[:, LANES * (GROUP * j + g):LANES * (GROUP * j + g + 1)], 0)
                 for g in range(GROUP)], axis=0)
            sink = jnp.concatenate(
                [jnp.full((Q_BLOCK, LANES), sink_ref[2 * (GROUP * j + g) + p], F32) for g in range(GROUP)],
                axis=0)
            s = jnp.where(mask, jnp.dot(qs, kt, preferred_element_type=F32), -jnp.inf)
            m = jnp.maximum(jnp.max(s, axis=-1, keepdims=True), sink)
            e = jnp.exp(s - jnp.concatenate([m, m], axis=1)).astype(BF16)
            pv = lax.dot_general(e, vt_ones, (((1,), (1,)), ((), ())), preferred_element_type=F32)
            o = pv[:, :LANES] / (pv[:, LANES:] + jnp.exp(sink - m))
            if p == 0:
                half_ref[...] = o
            else:
                lane_r = lax.broadcasted_iota(jnp.int32, (rows, LANES), 1)
                o_pair = jnp.where((lane_r // HEAD_DIM) == 0, half_ref[...], o).astype(BF16)
                for g in range(GROUP):
                    o_ref[:, LANES * (GROUP * j + g):LANES * (GROUP * j + g + 1)] = \
                        o_pair[Q_BLOCK * g:Q_BLOCK * (g + 1)]


def _swa_prompt(q, kvb, sink_perm):
    b, t, _ = q.shape
    nq = t // Q_BLOCK
    nck = kvb.shape[1]
    return pl.pallas_call(
        _swa_prompt_kernel,
        grid_spec=pltpu.PrefetchScalarGridSpec(
            num_scalar_prefetch=1,
            grid=(b, nq),
            in_specs=[
                pl.BlockSpec((None, Q_BLOCK, Q_DIM), lambda bi, qi, s: (bi, qi, 0)),
                pl.BlockSpec((None, nck, 2 * KV_DIM, LANES), lambda bi, qi, s: (bi, 0, 0, 0)),
            ],
            out_specs=pl.BlockSpec((None, Q_BLOCK, Q_DIM), lambda bi, qi, s: (bi, qi, 0)),
            scratch_shapes=[pltpu.VMEM((GROUP * Q_BLOCK, LANES), F32)],
        ),
        out_shape=jax.ShapeDtypeStruct((b, t, Q_DIM), BF16),
        compiler_params=_cparams(("arbitrary", "arbitrary")),
        name="swa_prompt",
    )(sink_perm, q, kvb)


def _swa_sample_kernel(qe_ref, sink_ref, kb_ref, vb_ref, kn_ref, vn_ref, o_ref, ko_ref, vo_ref,
                       *, n_seq, dt, past_len):
    r = qe_ref.shape[1]
    wb = kb_ref.shape[2]
    t_row = lax.broadcasted_iota(jnp.int32, (r, wb), 0) % dt
    col = lax.broadcasted_iota(jnp.int32, (r, wb), 1)
    dist_old = wb + t_row - col
    mask_old = (dist_old <= WINDOW) & (col >= wb - past_len)
    mask_new = (col >= wb - dt) & (col - (wb - dt) <= t_row)
    lane_k = lax.broadcasted_iota(jnp.int32, (KV_DIM, wb), 1)
    sink = sink_ref[...][:, 0:1]
    for i in range(n_seq):
        qe = qe_ref[i]
        kb, vb, kn, vn = kb_ref[i], vb_ref[i], kn_ref[i], vn_ref[i]
        s_old = jnp.where(mask_old, jnp.dot(qe, kb.astype(BF16), preferred_element_type=F32), -jnp.inf)
        s_new = jnp.where(mask_new, jnp.dot(qe, kn.astype(BF16), preferred_element_type=F32), -jnp.inf)
        m = jnp.maximum(jnp.maximum(jnp.max(s_old, axis=-1, keepdims=True),
                                    jnp.max(s_new, axis=-1, keepdims=True)), sink)
        e_old = jnp.exp(s_old - m)
        e_new = jnp.exp(s_new - m)
        denom = (jnp.sum(e_old, axis=-1, keepdims=True) + jnp.sum(e_new, axis=-1, keepdims=True)
                 + jnp.exp(sink - m))
        nt = (((1,), (1,)), ((), ()))
        o = (lax.dot_general(e_old.astype(BF16), vb.astype(BF16), nt, preferred_element_type=F32)
             + lax.dot_general(e_new.astype(BF16), vn.astype(BF16), nt, preferred_element_type=F32))
        o_ref[i] = o / denom
        keep = lane_k < wb - dt
        ko_ref[i] = jnp.where(keep, pltpu.roll(kb, wb - dt, 1), kn)
        vo_ref[i] = jnp.where(keep, pltpu.roll(vb, wb - dt, 1), vn)


def _swa_sample(qe, sink_rows, buf_kt, buf_vt, layer, knew, vnew, dt, past_len):
    db, r, _ = qe.shape
    wb = buf_kt.shape[-1]
    n_seq = 4 if db % 4 == 0 else 1
    seq_spec = lambda shape: pl.BlockSpec((n_seq,) + shape, lambda b: (b, 0, 0))
    buf_spec = pl.BlockSpec((None, n_seq, KV_DIM, wb), lambda b: (layer, b, 0, 0))
    return pl.pallas_call(
        functools.partial(_swa_sample_kernel, n_seq=n_seq, dt=dt, past_len=past_len),
        grid=(db // n_seq,),
        in_specs=[seq_spec((r, KV_DIM)), pl.BlockSpec((r, LANES), lambda b: (0, 0)),
                  buf_spec, buf_spec, seq_spec((KV_DIM, wb)), seq_spec((KV_DIM, wb))],
        out_specs=[seq_spec((r, KV_DIM)), seq_spec((KV_DIM, wb)), seq_spec((KV_DIM, wb))],
        out_shape=[jax.ShapeDtypeStruct((db, r, KV_DIM), F32),
                   jax.ShapeDtypeStruct((db, KV_DIM, wb), F32),
                   jax.ShapeDtypeStruct((db, KV_DIM, wb), F32)],
        compiler_params=_cparams(("arbitrary",)),
        name="swa_sample",
    )(qe, sink_rows, buf_kt, buf_vt, knew, vnew)


_GELU_C0 = 0.7978845608028654
_GELU_C1 = 0.7978845608028654 * 0.044715


def _gelu_tanh(x):
    t = jnp.tanh(x * (_GELU_C0 + _GELU_C1 * (x * x)))
    return x * (0.5 + 0.5 * t)


def _ffn_kernel(*refs, halo, stride, multi_tile, rb, tf, resident):
    if resident:
        (o_ref, x_ref, gm_ref, sh_ref, sc_ref, gf_ref, gpm_ref, gpre_ref, gpost_ref, wo_ref,
         wup_ref, cw_ref, cb_ref, wd_ref, prev_ref, out_ref, tail_ref,
         x1_ref, h_ref, act_ref, buf0_ref, buf1_ref, carry_ref) = refs
    else:
        (o_ref, x_ref, gm_ref, sh_ref, sc_ref, gf_ref, gpm_ref, gpre_ref, gpost_ref, wo_ref,
         wg_ref, wv_ref, cwg_ref, cwv_ref, cbg_ref, cbv_ref, wd_ref, pg_ref, pv_ref, out_ref, tg_ref, tv_ref,
         x1_ref, h_ref, act_ref, buf0_ref, buf1_ref, carry_ref) = refs
    ti = pl.program_id(1)
    nc = act_ref.shape[0]
    dff = wd_ref.shape[0]
    tm = x_ref.shape[0]
    bufs = (buf0_ref, buf1_ref)

    def pair(gate_val):
        return jnp.concatenate(gate_val, axis=1)

    def chunk_cols(cc, full_ref, g_ref, v_ref):
        if resident:
            return pair([full_ref[:, tf * cc:tf * (cc + 1)], full_ref[:, dff + tf * cc:dff + tf * (cc + 1)]])
        return pair([g_ref[...], v_ref[...]])

    def project(cc, buf):
        prev = (chunk_cols(cc, prev_ref, None, None) if resident else chunk_cols(cc, None, pg_ref, pv_ref))
        if multi_tile:
            first = jnp.full((halo, 2 * tf), ti, jnp.int32) == 0
            prev = jnp.where(first, prev, carry_ref[cc])
        buf[0:halo, :] = prev
        w = chunk_cols(cc, wup_ref, None, None) if resident else chunk_cols(cc, None, wg_ref, wv_ref)
        buf[halo:halo + tm, :] = jnp.dot(h_ref[...], w, preferred_element_type=F32)
        last = buf[tm:tm + halo, :]
        if multi_tile:
            carry_ref[cc] = last
        if resident:
            tail_ref[:, tf * cc:tf * (cc + 1)] = last[:, :tf]
            tail_ref[:, dff + tf * cc:dff + tf * (cc + 1)] = last[:, tf:]
        else:
            tg_ref[...] = last[:, :tf]
            tv_ref[...] = last[:, tf:]

    def activate(cc, buf):
        cw = chunk_cols(cc, cw_ref, None, None) if resident else chunk_cols(cc, None, cwg_ref, cwv_ref)
        cb = chunk_cols(cc, cb_ref, None, None) if resident else chunk_cols(cc, None, cbg_ref, cbv_ref)
        for r0 in range(0, tm, rb):
            u0 = buf[halo + r0:halo + r0 + rb, :]
            u1 = buf[halo - stride + r0:halo - stride + r0 + rb, :]
            u2 = buf[halo - 2 * stride + r0:halo - 2 * stride + r0 + rb, :]
            conv = cb + cw[0:1] * u2 + cw[1:2] * u1 + cw[2:3] * u0
            act_ref[cc, r0:r0 + rb, :] = (_gelu_tanh(conv[:, :tf]) * conv[:, tf:]).astype(BF16)

    halves = [(0, tm)] if tm % 16 else [(0, tm // 2), (tm // 2, tm)]

    def mod_rows(ref, lo, hi):
        m = ref[...]
        return m if m.shape[0] == 1 else _rows(m, tm)[lo:hi]

    def prologue():
        ys = [jnp.dot(o_ref[lo:hi, :], wo_ref[...], preferred_element_type=F32) for lo, hi in halves]
        for (lo, hi), y in zip(halves, ys):
            x1 = x_ref[lo:hi, :] + mod_rows(gm_ref, lo, hi) * _rms(y, gpm_ref[...])
            x1_ref[lo:hi, :] = x1
            h_ref[lo:hi, :] = (_rms(x1, gpre_ref[...]) * (1.0 + mod_rows(sc_ref, lo, hi))
                               + mod_rows(sh_ref, lo, hi)).astype(BF16)

    def epilogue():
        kd = (nc - 1) * tf
        for lo, hi in halves:
            y = jnp.dot(act_ref[nc - 1, lo:hi, :], wd_ref[kd:, :], preferred_element_type=F32)
            if nc > 1:
                early = jnp.concatenate([act_ref[k, lo:hi, :] for k in range(nc - 1)], axis=1)
                y = y + jnp.dot(early, wd_ref[:kd, :], preferred_element_type=F32)
            out_ref[lo:hi, :] = x1_ref[lo:hi, :] + mod_rows(gf_ref, lo, hi) * _rms(y, gpost_ref[...])

    if resident:
        prologue()
        project(0, bufs[0])
        for cc in range(1, nc):
            activate(cc - 1, bufs[(cc - 1) % 2])
            project(cc, bufs[cc % 2])
        activate(nc - 1, bufs[(nc - 1) % 2])
        epilogue()
        return

    c = pl.program_id(2)

    @pl.when(c == 0)
    def _():
        prologue()
        project(c, bufs[0])

    for parity in range(2):
        @pl.when((c > 0) & (c < nc) & (lax.rem(c, 2) == parity))
        def _():
            activate(c - 1, bufs[1 - parity])
            project(c, bufs[parity])

    @pl.when(c == nc)
    def _():
        activate(c - 1, bufs[(nc - 1) % 2])
        epilogue()


def _ffn(o, x, gate_m, shift, scale, gate_f, gpost_m, gpre, gpost, wo, w_up, cw, cb, wd, prev,
         *, tm, tf, halo, stride, mod_map, resident):
    nb, r, d = x.shape
    dff = wd[0].shape[-2]
    nc = dff // tf
    nt = r // tm
    mr = shift[0].shape[-2]
    slabs = (gate_m, shift, scale, gate_f, gpost_m, gpre, gpost, wo, w_up, cw, cb, wd)
    gate_m_a, shift_a, scale_a, gate_f_a, gpost_m_a, gpre_a, gpost_a, wo_a, w_up_a, cw_a, cb_a, wd_a = (
        s[0] for s in slabs)
    multi_tile = nt > 1
    rb = 64 if tm % 64 == 0 else tm
    once = pl.Buffered(1)
    row_mode = dict(pipeline_mode=once) if nb * nt == 1 else {}
    kern = functools.partial(_ffn_kernel, halo=halo, stride=stride, multi_tile=multi_tile, rb=rb, tf=tf,
                             resident=resident)
    scratch = [
        pltpu.VMEM((tm, d), F32), pltpu.VMEM((tm, d), BF16), pltpu.VMEM((nc, tm, tf), BF16),
        pltpu.VMEM((halo + tm, 2 * tf), F32), pltpu.VMEM((halo + tm, 2 * tf), F32),
        pltpu.VMEM((nc, halo, 2 * tf) if multi_tile else (1, 8, LANES), F32),
    ]
    out_sds = jax.ShapeDtypeStruct((nb, r, d), F32)
    if resident:
        im = lambda b, t: (b, t, 0)
        seq = lambda b, t: (b, 0, 0)
        const = lambda b, t: (0, 0)
        mod_spec = lambda p: _slab(p, (None, mr, d), lambda b, t: (mod_map(b), 0, 0))
        vec_spec = lambda p: _slab(p, (1, d), const)
        whole = lambda p, **kw: _slab(p, p[0].shape[-2:], const, **kw)
        out, tail = pl.pallas_call(
            kern, grid=(nb, nt),
            in_specs=[
                pl.BlockSpec((None, tm, Q_DIM), im, **row_mode), pl.BlockSpec((None, tm, d), im, **row_mode),
                mod_spec(gate_m), mod_spec(shift), mod_spec(scale), mod_spec(gate_f),
                vec_spec(gpost_m), vec_spec(gpre), vec_spec(gpost),
                whole(wo, pipeline_mode=once), whole(w_up, pipeline_mode=once), whole(cw), whole(cb),
                whole(wd, pipeline_mode=once),
                pl.BlockSpec((None, halo, 2 * dff), seq),
            ],
            out_specs=[pl.BlockSpec((None, tm, d), im), pl.BlockSpec((None, halo, 2 * dff), seq)],
            out_shape=[out_sds, jax.ShapeDtypeStruct((nb, halo, 2 * dff), F32)],
            scratch_shapes=scratch,
            compiler_params=_cparams(("arbitrary", "arbitrary")),
            name="mixer_out_conv_ffn",
        )(o, x, gate_m_a, shift_a, scale_a, gate_f_a, gpost_m_a, gpre_a, gpost_a, wo_a, w_up_a, cw_a, cb_a,
          wd_a, prev)
        return out, tail

    assert not multi_tile
    last = nc - 1
    im = lambda b, t, c: (b, t, 0)
    const = lambda b, t, c: (0, 0)
    mod_spec = lambda p: _slab(p, (None, mr, d), lambda b, t, c: (mod_map(b), 0, 0))
    vec_spec = lambda p: _slab(p, (1, d), const)
    whole = lambda p, **kw: _slab(p, p[0].shape[-2:], const, **kw)
    proj = lambda p, off: _slab(p, (p[0].shape[-2], tf), lambda b, t, c: (0, off + jnp.minimum(c, last)))
    actv = lambda p, off: _slab(p, (p[0].shape[-2], tf), lambda b, t, c: (0, off + jnp.maximum(c - 1, 0)))
    halo_spec = lambda off: pl.BlockSpec((None, halo, tf), lambda b, t, c: (b, 0, off + jnp.minimum(c, last)))
    tail_sds = jax.ShapeDtypeStruct((nb, halo, dff), F32)
    out, tail_g, tail_v = pl.pallas_call(
        kern, grid=(nb, nt, nc + 1),
        in_specs=[
            pl.BlockSpec((None, tm, Q_DIM), im, **row_mode), pl.BlockSpec((None, tm, d), im, **row_mode),
            mod_spec(gate_m), mod_spec(shift), mod_spec(scale), mod_spec(gate_f),
            vec_spec(gpost_m), vec_spec(gpre), vec_spec(gpost),
            whole(wo, pipeline_mode=once),
            proj(w_up, 0), proj(w_up, nc), actv(cw, 0), actv(cw, nc), actv(cb, 0), actv(cb, nc),
            whole(wd, pipeline_mode=once),
            halo_spec(0), halo_spec(nc),
        ],
        out_specs=[pl.BlockSpec((None, tm, d), im), halo_spec(0), halo_spec(0)],
        out_shape=[out_sds, tail_sds, tail_sds],
        scratch_shapes=scratch,
        compiler_params=_cparams(("arbitrary", "arbitrary", "arbitrary")),
        name="mixer_out_conv_ffn_chunked",
    )(o, x, gate_m_a, shift_a, scale_a, gate_f_a, gpost_m_a, gpre_a, gpost_a, wo_a, w_up_a, w_up_a,
      cw_a, cw_a, cb_a, cb_a, wd_a, prev, prev)
    return out, jnp.concatenate([tail_g, tail_v], axis=-1)


def _rope_tables(pos):
    inv_freq = jnp.power(jnp.float32(ROPE_THETA), -jnp.arange(ROT_HALF, dtype=F32) * (2.0 / ROT_DIM))
    ang = pos.astype(F32)[:, None] * inv_freq[None, :]
    cos, sin = jnp.cos(ang), jnp.sin(ang)
    n = pos.shape[0]
    ones = jnp.ones((n, HEAD_DIM - ROT_DIM), F32)
    zeros_r = jnp.zeros((n, HEAD_DIM - ROT_DIM), F32)
    zeros_h = jnp.zeros((n, ROT_HALF), F32)
    cq = jnp.concatenate([cos, cos, ones], axis=1)
    s1 = jnp.concatenate([zeros_h, sin, zeros_r], axis=1)
    s2 = jnp.concatenate([-sin, zeros_h, zeros_r], axis=1)
    rep = LANES // HEAD_DIM
    return (jnp.tile(cq, (1, rep)), jnp.tile(s1, (1, rep)), jnp.tile(s2, (1, rep)), cos.T, sin.T)


def _kv_onehot():
    oh = np.zeros((N_QBLK, 2, N_KV_HEADS), np.float32)
    for i in range(N_QBLK):
        for p in range(2):
            oh[i, p, 2 * (i // GROUP) + p] = 1.0
    return oh


def _expand_q(q_tm):
    dt, db, _ = q_tm.shape
    q5 = q_tm.reshape(dt, db, N_QBLK, 2, HEAD_DIM).transpose(1, 2, 3, 0, 4)
    oh = jnp.asarray(_kv_onehot(), q_tm.dtype)
    q6 = q5[:, :, :, :, None, :] * oh[None, :, :, None, :, None]
    return q6.reshape(db, N_QBLK * 2 * dt, KV_DIM)


def _select_o(o_full, dt):
    db = o_full.shape[0]
    o6 = o_full.reshape(db, N_QBLK, 2, dt, N_KV_HEADS, HEAD_DIM)
    o5 = jnp.sum(o6 * jnp.asarray(_kv_onehot())[None, :, :, None, :, None], axis=4)
    return o5.transpose(3, 0, 1, 2, 4).reshape(dt, db, Q_DIM).astype(BF16)


def _rows_of_heads(v, dt):
    vp = v[jnp.asarray(HEAD_PERM)]
    return jnp.broadcast_to(vp[:, None, None], (N_HEADS, dt, LANES)).reshape(N_HEADS * dt, LANES).astype(F32)


def _from_t(kt, lead):
    n = kt.shape[-1]
    x = kt.reshape(lead + (N_KV_HEADS, HEAD_DIM, n))
    nd = len(lead)
    return x.transpose(tuple(range(nd)) + (nd + 2, nd, nd + 1))


def kernel(x_prompt, x_sample, cache_sb_k, cache_sb_v, state_swa_k, state_swa_v, state_conv, page_table,
           c_prompt, c_sample, mod_w, mod_b, norm_mix_pre, norm_mix_post, norm_ffn_pre, norm_ffn_post,
           sb_w_qkv, sb_w_o, sb_bias, swa_w_qkv, swa_w_o, swa_sinks, ffn_w_up, ffn_conv_w, ffn_conv_b,
           ffn_w_down):
    b, t, d = x_prompt.shape
    db, dt, _ = x_sample.shape
    depth = mod_w.shape[0]
    n_pages = page_table.shape[1]
    past_len = n_pages * PAGE_SIZE
    wb = state_swa_k.shape[2]
    dff = ffn_w_down.shape[1]
    n_phys = cache_sb_k.shape[1]

    tm_p = min(512, t)
    tm_f = min(512, t)
    tf = 256

    mod = _modulation(jnp.concatenate([c_prompt, c_sample], axis=0), mod_w, mod_b)
    mod_p = mod[:, :, :b].reshape(depth, N_MOD, b, 1, d)
    mod_s = mod[:, :, b:].reshape(depth, N_MOD, 1, db, d)

    cache_kt = cache_sb_k.transpose(0, 1, 3, 4, 2).reshape(-1, n_phys, KV_DIM, PAGE_SIZE)
    cache_vt = cache_sb_v.transpose(0, 1, 3, 4, 2).reshape(-1, n_phys, KV_DIM, PAGE_SIZE)
    swa_kt = state_swa_k.transpose(0, 1, 3, 4, 2).reshape(-1, db, KV_DIM, wb)
    swa_vt = state_swa_v.transpose(0, 1, 3, 4, 2).reshape(-1, db, KV_DIM, wb)

    qcols = jnp.asarray(np.concatenate([np.arange(HEAD_DIM) + HEAD_DIM * h for h in HEAD_PERM]))
    perm = jnp.asarray(HEAD_PERM)

    def prep_mixer(w_qkv, w_o):
        return (w_qkv[:, :, :Q_DIM][:, :, qcols].astype(BF16),
                w_qkv[:, :, Q_DIM:].transpose(0, 2, 1).astype(BF16),
                w_o[:, qcols, :].astype(BF16))
    mixer_w = (prep_mixer(sb_w_qkv, sb_w_o), prep_mixer(swa_w_qkv, swa_w_o))
    w_up_all = ffn_w_up.astype(BF16)
    wd_all = ffn_w_down.astype(BF16)
    cb_all = ffn_conv_b.reshape(depth, 1, 2 * dff)
    gains = [g.reshape(depth, 1, d) for g in (norm_mix_pre, norm_mix_post, norm_ffn_pre, norm_ffn_post)]

    rope_p = _rope_tables(jnp.arange(t))
    rope_p = tuple(a[None] for a in rope_p) + (False,)
    cq, s1, s2, ck, sk = _rope_tables(past_len + jnp.arange(dt))
    rope_s = (cq[:, None, :], s1[:, None, :], s2[:, None, :],
              jnp.broadcast_to(ck.T[:, :, None], (dt, ROT_HALF, db)),
              jnp.broadcast_to(sk.T[:, :, None], (dt, ROT_HALF, db)), True)

    xp = x_prompt
    xs = x_sample.transpose(1, 0, 2)
    pmap = lambda bi: bi
    smap = lambda bi: 0

    outs = {k: [] for k in ("sbkp", "sbvp", "sbks", "sbvs", "swkp", "swvp", "swks", "swvs", "cp", "cs")}
    for i in range(depth):
        j = i // 2
        is_sb = i % 2 == 0
        wq, wkvt, wo = ((w, (j,)) for w in mixer_w[0 if is_sb else 1])
        g_pre, g_post, gf_pre, gf_post = ((g, (i,)) for g in gains)
        mp = [(mod_p, (i, k)) for k in range(N_MOD)]
        ms = [(mod_s, (i, k)) for k in range(N_MOD)]

        qp, ktp, vtp, kvbp = _qkv(xp, mp[0], mp[1], g_pre, wq, wkvt, None if is_sb else rope_p,
                                  tm=tm_p, mod_map=pmap)
        qs, kts, vts, _ = _qkv(xs, ms[0], ms[1], g_pre, wq, wkvt, None if is_sb else rope_s,
                               tm=db, mod_map=smap)
        qe = _expand_q(qs)
        kn = kts.transpose(2, 1, 0)
        vn = vts.transpose(2, 1, 0)
        if is_sb:
            op = _sb_prompt(qp, kvbp, sb_bias[j][perm])
            pad = ((0, 0), (0, 0), (0, LANES - dt))
            o_full = _sb_sample(page_table, qe, _rows_of_heads(sb_bias[j], dt),
                                jnp.pad(kn, pad).astype(BF16), jnp.pad(vn, pad).astype(BF16),
                                cache_kt, cache_vt, j, dt)
            outs["sbkp"].append(_from_t(ktp, (b,)))
            outs["sbvp"].append(_from_t(vtp, (b,)))
            outs["sbks"].append(_from_t(kn, (db,)))
            outs["sbvs"].append(_from_t(vn, (db,)))
        else:
            op = _swa_prompt(qp, kvbp, swa_sinks[j][perm])
            pad = ((0, 0), (0, 0), (wb - dt, 0))
            o_full, ko, vo = _swa_sample(qe, _rows_of_heads(swa_sinks[j], dt), swa_kt, swa_vt, j,
                                         jnp.pad(kn, pad), jnp.pad(vn, pad), dt, past_len)
            keep = min(WINDOW, t)
            outs["swkp"].append(_from_t(ktp[:, :, t - keep:], (b,)))
            outs["swvp"].append(_from_t(vtp[:, :, t - keep:], (b,)))
            outs["swks"].append(_from_t(ko, (db,)))
            outs["swvs"].append(_from_t(vo, (db,)))
        os_ = _select_o(o_full, dt)

        w_up, wd, cw, cb = (w_up_all, (i,)), (wd_all, (i,)), (ffn_conv_w, (i,)), (cb_all, (i,))

        halo_p = 8
        zero_prev = jnp.zeros((b, halo_p, 2 * dff), F32)
        xp, tail_p = _ffn(op, xp, mp[2], mp[3], mp[4], mp[5], g_post, gf_pre, gf_post, wo,
                          w_up, cw, cb, wd, zero_prev,
                          tm=tm_f, tf=tf, halo=halo_p, stride=1, mod_map=pmap, resident=True)
        st = state_conv[i].transpose(1, 0, 2).reshape(1, 2 * db, 2 * dff)
        xs1, tail_s = _ffn(os_.reshape(1, dt * db, Q_DIM), xs.reshape(1, dt * db, d),
                           ms[2], ms[3], ms[4], ms[5], g_post, gf_pre, gf_post, wo,
                           w_up, cw, cb, wd, st,
                           tm=dt * db, tf=tf, halo=2 * db, stride=db, mod_map=smap, resident=False)
        xs = xs1.reshape(dt, db, d)
        outs["cp"].append(tail_p[:, halo_p - 2:])
        outs["cs"].append(tail_s.reshape(2, db, 2 * dff).transpose(1, 0, 2))

    st_ = lambda k: jnp.stack(outs[k])
    return (xp, xs.transpose(1, 0, 2),
            st_("sbkp"), st_("sbvp"), st_("sbks"), st_("sbvs"),
            st_("swkp"), st_("swvp"), st_("swks"), st_("swvs"),
            st_("cp"), st_("cs"))
```

```python
import functools

import numpy as np
import jax
import jax.numpy as jnp
from jax import lax
from jax.experimental import pallas as pl
from jax.experimental.pallas import tpu as pltpu

F32 = jnp.float32
BF16 = jnp.bfloat16

HEAD_DIM = 64
N_HEADS = 16
N_KV_HEADS = 4
GROUP = N_HEADS // N_KV_HEADS
Q_DIM = N_HEADS * HEAD_DIM
KV_DIM = N_KV_HEADS * HEAD_DIM
PAGE_SIZE = 128
WINDOW = 128
Q_BLOCK = 128
ROT_DIM = HEAD_DIM // 4
ROT_HALF = ROT_DIM // 2
ROPE_THETA = 500000.0
RMS_EPS = 1e-6
N_MOD = 6
LANES = 128
N_QBLK = Q_DIM // LANES
N_PAIR = KV_DIM // LANES
VMEM_LIMIT = 56 * 1024 * 1024

HEAD_PERM = [4 * (2 * (i // 4) + p) + (i % 4) for i in range(N_QBLK) for p in range(2)]


def _cparams(sem):
    return pltpu.CompilerParams(dimension_semantics=sem, vmem_limit_bytes=VMEM_LIMIT)


def _rms(x, g):
    return x * lax.rsqrt(jnp.mean(x * x, axis=-1, keepdims=True) + RMS_EPS) * g


def _rows(m, rows):
    mr = m.shape[0]
    if mr == 1 or mr == rows:
        return m
    return jnp.concatenate([m] * (rows // mr), axis=0)


def _norm_mod(x, g, shift, scale):
    rows = x.shape[0]
    return _rms(x, g) * (1.0 + _rows(scale, rows)) + _rows(shift, rows)


def _slab(param, tail_block, tail_index, **kw):
    _, lead = param
    lead = tuple(lead)
    return pl.BlockSpec((None,) * len(lead) + tuple(tail_block),
                        lambda *grid: lead + tuple(tail_index(*grid)), **kw)


def _mod_kernel(c_ref, w_ref, b_ref, o_ref):
    c = c_ref[...]
    s = c / (1.0 + jnp.exp(-c))
    o_ref[...] = jnp.dot(s.astype(BF16), w_ref[...].astype(BF16), preferred_element_type=F32) + b_ref[...]


def _modulation(c_all, mod_w, mod_b):
    depth, d, _ = mod_w.shape
    n = c_all.shape[0]
    return pl.pallas_call(
        _mod_kernel,
        grid=(depth, N_MOD),
        in_specs=[
            pl.BlockSpec((n, d), lambda i, k: (0, 0)),
            pl.BlockSpec((None, d, d), lambda i, k: (i, 0, k)),
            pl.BlockSpec((None, None, 1, d), lambda i, k: (i, k, 0, 0)),
        ],
        out_specs=pl.BlockSpec((None, None, n, d), lambda i, k: (i, k, 0, 0)),
        out_shape=jax.ShapeDtypeStruct((depth, N_MOD, n, d), F32),
        compiler_params=_cparams(("arbitrary", "arbitrary")),
        name="modulation",
    )(c_all, mod_w, mod_b.reshape(depth, N_MOD, 1, d))


def _qkv_kernel(*refs, rope, kc):
    if rope:
        (x_ref, sh_ref, sc_ref, g_ref, wq_ref, wkv_ref, cq_ref, s1_ref, s2_ref, ck_ref, sk_ref,
         q_ref, kt_ref, vt_ref, kvb_ref) = refs
    else:
        x_ref, sh_ref, sc_ref, g_ref, wq_ref, wkv_ref, q_ref, kt_ref, vt_ref, kvb_ref = refs
    tm = x_ref.shape[0]
    h = _norm_mod(x_ref[...], g_ref[...], sh_ref[...], sc_ref[...]).astype(BF16)
    q = jnp.dot(h, wq_ref[...], preferred_element_type=F32)
    kvt = lax.dot_general(wkv_ref[...], h, (((1,), (1,)), ((), ())),
                          preferred_element_type=F32)
    scale = HEAD_DIM ** -0.5
    if rope:
        cq, s1, s2 = cq_ref[...], s1_ref[...], s2_ref[...]
        for i in range(N_QBLK):
            xr = q[:, LANES * i:LANES * (i + 1)]
            rot = xr * cq + pltpu.roll(xr, ROT_HALF, 1) * s1 + pltpu.roll(xr, LANES - ROT_HALF, 1) * s2
            q_ref[:, LANES * i:LANES * (i + 1)] = (rot * scale).astype(BF16)
        ck, sk = ck_ref[...], sk_ref[...]
        pieces = []
        for c in range(N_KV_HEADS):
            base = HEAD_DIM * c
            x1 = kvt[base:base + ROT_HALF]
            x2 = kvt[base + ROT_HALF:base + ROT_DIM]
            pieces += [x1 * ck - x2 * sk, x2 * ck + x1 * sk, kvt[base + ROT_DIM:base + HEAD_DIM]]
        kt = jnp.concatenate(pieces, axis=0)
    else:
        q_ref[...] = (q * scale).astype(BF16)
        kt = kvt[:KV_DIM]
    vt = kvt[KV_DIM:]
    kt_ref[...] = kt
    vt_ref[...] = vt
    kb = kt.astype(BF16)
    vb = vt.astype(BF16)
    for c in range(tm // kc):
        kvb_ref[c, :KV_DIM, :] = kb[:, kc * c:kc * (c + 1)]
        kvb_ref[c, KV_DIM:, :] = vb[:, kc * c:kc * (c + 1)]


def _qkv(x, shift, scale, g, wq, wkvt, rope_tabs, *, tm, mod_map):
    nb, r, d = x.shape
    kc = min(LANES, tm)
    nt = r // tm
    rope = rope_tabs is not None
    mr = shift[0].shape[-2]
    const = lambda b, t: (0, 0)
    mod_index = lambda b, t: (mod_map(b), 0, 0)
    in_specs = [
        pl.BlockSpec((None, tm, d), lambda b, t: (b, t, 0)),
        _slab(shift, (None, mr, d), mod_index),
        _slab(scale, (None, mr, d), mod_index),
        _slab(g, (1, d), const),
        _slab(wq, wq[0].shape[-2:], const),
        _slab(wkvt, wkvt[0].shape[-2:], const),
    ]
    args = [x, shift[0], scale[0], g[0], wq[0], wkvt[0]]
    if rope:
        cq, s1, s2, ck, sk, per_b = rope_tabs
        if per_b:
            qspec = pl.BlockSpec((None, 1, LANES), lambda b, t: (b, 0, 0))
            kspec = pl.BlockSpec((None, ROT_HALF, tm), lambda b, t: (b, 0, 0))
        else:
            qspec = pl.BlockSpec((None, tm, LANES), lambda b, t: (0, t, 0))
            kspec = pl.BlockSpec((None, ROT_HALF, tm), lambda b, t: (0, 0, t))
        in_specs += [qspec, qspec, qspec, kspec, kspec]
        args += [cq, s1, s2, ck, sk]
    out_shape = [
        jax.ShapeDtypeStruct((nb, r, Q_DIM), BF16),
        jax.ShapeDtypeStruct((nb, KV_DIM, r), F32),
        jax.ShapeDtypeStruct((nb, KV_DIM, r), F32),
        jax.ShapeDtypeStruct((nb, r // kc, 2 * KV_DIM, kc), BF16),
    ]
    out_specs = [
        pl.BlockSpec((None, tm, Q_DIM), lambda b, t: (b, t, 0)),
        pl.BlockSpec((None, KV_DIM, tm), lambda b, t: (b, 0, t)),
        pl.BlockSpec((None, KV_DIM, tm), lambda b, t: (b, 0, t)),
        pl.BlockSpec((None, tm // kc, 2 * KV_DIM, kc), lambda b, t: (b, t, 0, 0)),
    ]
    return pl.pallas_call(
        functools.partial(_qkv_kernel, rope=rope, kc=kc),
        grid=(nb, nt),
        in_specs=in_specs,
        out_specs=out_specs,
        out_shape=out_shape,
        compiler_params=_cparams(("arbitrary", "arbitrary")),
        name="qkv_rope" if rope else "qkv",
    )(*args)


LOG2E = 1.4426950408889634


def _sb_split(z, mask):
    n = z.shape[1] // LANES
    sp = jnp.maximum(z, 0.0) + jnp.log(1.0 + jnp.exp2(jnp.abs(z) * (-LOG2E)))
    if mask is not None:
        sp = jnp.where(mask, sp, 0.0)
    hi_f = lax.bitcast_convert_type(
        lax.bitcast_convert_type(sp, jnp.uint32) & jnp.uint32(0xFFFF0000), F32)
    hi = hi_f.astype(BF16)
    lo = (sp - hi_f).astype(BF16)
    return jnp.concatenate(
        [jnp.concatenate([hi[:, LANES * k:LANES * (k + 1)], lo[:, LANES * k:LANES * (k + 1)]], axis=1)
         for k in range(n)], axis=0)


def _sb_log_weights(z, t, carry):
    r = z.shape[0]
    n = z.shape[1] // LANES
    with_totals = t.shape[1] == 2 * LANES
    tails = [None] * n
    for k in reversed(range(n)):
        blk = t[r * k:r * (k + 1)]
        tails[k] = blk[:, :LANES] + carry
        if with_totals:
            carry = carry + blk[:, LANES:]
        else:
            carry = jnp.broadcast_to(tails[k][:, 0:1], carry.shape)
    tail = tails[0] if n == 1 else jnp.concatenate(tails, axis=1)
    return z + tail, carry


def _sb_finish(z, t, mask, carry):
    lw, carry = _sb_log_weights(z, t, carry)
    a = jnp.exp(lw)
    if mask is not None:
        a = jnp.where(mask, a, 0.0)
    return a.astype(BF16), carry


def _cumsum_total_matrix():
    j = np.arange(LANES)[:, None]
    c = np.arange(LANES)[None, :]
    half = np.concatenate([-(j >= c).astype(np.float32), -np.ones((LANES, LANES), np.float32)], axis=1)
    return jnp.asarray(np.concatenate([half, half], axis=0), dtype=BF16)


def _sb_prompt_kernel(bias_ref, q_ref, kvb_ref, w_ref, o_ref, *scratch):
    qi = pl.program_id(1)
    rows = GROUP * Q_BLOCK
    lane = lax.broadcasted_iota(jnp.int32, (Q_BLOCK, LANES), 1)
    top = qi // 2
    qs_refs, c_refs, acc_refs = scratch[0:4], scratch[4:8], scratch[8:12]
    chains = [(j, p) for j in range(N_PAIR) for p in range(2)]
    nt_dims = (((1,), (1,)), ((), ()))
    for n, (j, p) in enumerate(chains):
        in_half = (lane // HEAD_DIM) == p
        for g in range(GROUP):
            blk = q_ref[:, LANES * (GROUP * j + g):LANES * (GROUP * j + g + 1)]
            qs_refs[n][Q_BLOCK * g:Q_BLOCK * (g + 1), :] = jnp.where(in_half, blk, 0)
        c_refs[n][...] = jnp.zeros_like(c_refs[n])
        acc_refs[n][...] = jnp.zeros_like(acc_refs[n])

    def chunk(ci, masked):
        kts = [jnp.concatenate([kvb_ref[2 * ci, LANES * j:LANES * (j + 1), :],
                                kvb_ref[2 * ci + 1, LANES * j:LANES * (j + 1), :]], axis=1)
               for j in range(N_PAIR)]
        vts = [jnp.concatenate([kvb_ref[2 * ci, KV_DIM + LANES * j:KV_DIM + LANES * (j + 1), :],
                                kvb_ref[2 * ci + 1, KV_DIM + LANES * j:KV_DIM + LANES * (j + 1), :]], axis=1)
               for j in range(N_PAIR)]
        mask = None
        if masked:
            r_idx = lax.broadcasted_iota(jnp.int32, (rows, 2 * LANES), 0) & (Q_BLOCK - 1)
            c_idx = lax.broadcasted_iota(jnp.int32, (rows, 2 * LANES), 1)
            mask = (ci * 2 * LANES + c_idx) < (qi * Q_BLOCK + r_idx)
        zs = [jnp.dot(qs_refs[n][...], kts[j], preferred_element_type=F32) for n, (j, p) in enumerate(chains)]
        zs = [jnp.concatenate(
            [zs[n][Q_BLOCK * g:Q_BLOCK * (g + 1)] + bias_ref[2 * (GROUP * j + g) + p]
             for g in range(GROUP)], axis=0) for n, (j, p) in enumerate(chains)]
        hls = [_sb_split(z, mask) for z in zs]
        ts = [jnp.dot(hl, w_ref[...], preferred_element_type=F32) for hl in hls]
        aw = [_sb_finish(zs[n], ts[n], mask, c_refs[n][...]) for n in range(len(chains))]
        for n in range(len(chains)):
            c_refs[n][...] = aw[n][1]
        pv = [lax.dot_general(aw[n][0], vts[j], nt_dims, preferred_element_type=F32)
              for n, (j, p) in enumerate(chains)]
        for n in range(len(chains)):
            acc_refs[n][...] += pv[n]

    chunk(top, True)

    def body(n, _):
        chunk(top - 1 - n, False)
        return 0

    lax.fori_loop(0, top, body, 0)
    lane_r = lax.broadcasted_iota(jnp.int32, (rows, LANES), 1)
    for j in range(N_PAIR):
        o_pair = jnp.where((lane_r // HEAD_DIM) == 0, acc_refs[2 * j][...],
                           acc_refs[2 * j + 1][...]).astype(BF16)
        for g in range(GROUP):
            o_ref[:, LANES * (GROUP * j + g):LANES * (GROUP * j + g + 1)] = \
                o_pair[Q_BLOCK * g:Q_BLOCK * (g + 1)]


def _sb_prompt(q, kvb, bias_perm):
    b, t, _ = q.shape
    nq = t // Q_BLOCK
    nck = kvb.shape[1]
    rows = GROUP * Q_BLOCK
    return pl.pallas_call(
        _sb_prompt_kernel,
        grid_spec=pltpu.PrefetchScalarGridSpec(
            num_scalar_prefetch=1,
            grid=(b, nq),
            in_specs=[
                pl.BlockSpec((None, Q_BLOCK, Q_DIM), lambda bi, qi, s: (bi, qi, 0)),
                pl.BlockSpec((None, nck, 2 * KV_DIM, LANES), lambda bi, qi, s: (bi, 0, 0, 0)),
                pl.BlockSpec((2 * LANES, LANES), lambda bi, qi, s: (0, 0)),
            ],
            out_specs=pl.BlockSpec((None, Q_BLOCK, Q_DIM), lambda bi, qi, s: (bi, qi, 0)),
            scratch_shapes=([pltpu.VMEM((rows, LANES), BF16)] * N_KV_HEADS
                            + [pltpu.VMEM((rows, LANES), F32)] * (2 * N_KV_HEADS)),
        ),
        out_shape=jax.ShapeDtypeStruct((b, t, Q_DIM), BF16),
        compiler_params=_cparams(("arbitrary", "arbitrary")),
        name="sb_prompt",
    )(bias_perm, q, kvb, _cumsum_total_matrix()[:, :LANES])


def _sb_sample_kernel(*refs, n_pp, dt, steps, n_seq, layer):
    (sched_ref, qe_ref, bias_ref, w_ref, kn_ref, vn_ref, ck_ref, cv_ref,
     o_ref, c_ref, acc_ref, lw0_ref, lw1_ref, kbuf_ref, vbuf_ref, ksem, vsem) = refs
    lws = (lw0_ref, lw1_ref)
    g = pl.program_id(0)
    last = n_seq * steps
    seq_a = g // steps
    seq_b = (g - 1) // steps
    r = qe_ref.shape[0]
    nt_dims = (((1,), (1,)), ((), ()))
    halves = 2 if n_pp % 2 == 0 else 1

    def page_copy(cache_ref, buf_ref, sem, page, slot, k):
        return pltpu.make_async_copy(cache_ref.at[layer, page], buf_ref.at[slot, k], sem.at[slot])

    def start_pages(cache_ref, buf_ref, sem, group, slot):
        for k in range(n_pp):
            page_copy(cache_ref, buf_ref, sem, sched_ref[group * n_pp + k], slot, k).start()

    def wait_pages(cache_ref, buf_ref, sem, slot):
        for k in range(n_pp):
            page_copy(cache_ref, buf_ref, sem, 0, slot, k).wait()

    def logits(slot):
        qe, bias = qe_ref[...], bias_ref[...]
        return jnp.concatenate(
            [jnp.dot(qe, kbuf_ref[slot, k].astype(BF16), preferred_element_type=F32) + bias
             for k in range(n_pp)], axis=1)

    def log_weights(z, lw_ref):
        hl = _sb_split(z, None)
        hr = hl.shape[0] // halves
        t = jnp.concatenate([jnp.dot(hl[i:i + hr], w_ref[...], preferred_element_type=F32)
                             for i in range(0, hl.shape[0], hr)], axis=0)
        lw, c_new = _sb_log_weights(z, t, c_ref[...])
        c_ref[...] = c_new
        lw_ref[...] = lw

    def weights_v(a, slot):
        hk = n_pp // halves
        pv = None
        for k0 in range(0, n_pp, hk):
            vt = jnp.concatenate([vbuf_ref[slot, k].astype(BF16) for k in range(k0, k0 + hk)], axis=1)
            part = lax.dot_general(a[:, PAGE_SIZE * k0:PAGE_SIZE * (k0 + hk)], vt, nt_dims,
                                   preferred_element_type=F32)
            pv = part if pv is None else pv + part
        return pv

    def accumulate(pv):
        slot = lax.rem(seq_b, 2)
        acc = acc_ref[slot] + pv
        acc_ref[slot] = acc
        o_ref[...] = acc

    @pl.when((lax.rem(g, steps) == 0) & (g < last))
    def _():
        t_row = lax.broadcasted_iota(jnp.int32, (r, LANES), 0) % dt
        t_key = lax.broadcasted_iota(jnp.int32, (r, LANES), 1)
        z = jnp.dot(qe_ref[...], kn_ref[...], preferred_element_type=F32) + bias_ref[...]
        mask = t_key < t_row
        t = jnp.dot(_sb_split(z, mask), w_ref[...], preferred_element_type=F32)
        a, c_new = _sb_finish(z, t, mask, jnp.zeros((r, LANES), F32))
        c_ref[...] = c_new
        acc_ref[lax.rem(seq_a, 2)] = lax.dot_general(a, vn_ref[...], nt_dims, preferred_element_type=F32)

    @pl.when(g == 0)
    def _():
        start_pages(ck_ref, kbuf_ref, ksem, 0, 0)
        start_pages(ck_ref, kbuf_ref, ksem, min(1, last - 1), 1)
        start_pages(cv_ref, vbuf_ref, vsem, 0, 0)
        wait_pages(ck_ref, kbuf_ref, ksem, 0)
        log_weights(logits(0), lws[0])

    for parity in range(2):
        @pl.when((g > 0) & (g < last) & (lax.rem(g, 2) == parity))
        def _():
            start_pages(ck_ref, kbuf_ref, ksem, jnp.minimum(g + 1, last - 1), 1 - parity)
            start_pages(cv_ref, vbuf_ref, vsem, g, parity)
            wait_pages(ck_ref, kbuf_ref, ksem, parity)
            wait_pages(cv_ref, vbuf_ref, vsem, 1 - parity)
            a = jnp.exp(lws[1 - parity][...]).astype(BF16)
            z = logits(parity)
            pv = weights_v(a, 1 - parity)
            log_weights(z, lws[parity])
            accumulate(pv)

    @pl.when(g == last)
    def _():
        wait_pages(ck_ref, kbuf_ref, ksem, last % 2)
        wait_pages(cv_ref, vbuf_ref, vsem, (last - 1) % 2)
        accumulate(weights_v(jnp.exp(lws[(last - 1) % 2][...]).astype(BF16), (last - 1) % 2))


def _sb_sample(page_table, qe, bias_rows, knew, vnew, cache_kt, cache_vt, layer, dt):
    db, r, _ = qe.shape
    n_pages = page_table.shape[1]
    n_pp = min(16, n_pages)
    while n_pages % n_pp:
        n_pp -= 1
    steps = n_pages // n_pp
    last = db * steps
    sched = page_table.reshape(db, steps, n_pp)[:, ::-1, :].reshape(last * n_pp)

    seq_a = lambda g, sc: (jnp.minimum(g // steps, db - 1), 0, 0)
    in_specs = [
        pl.BlockSpec((None, r, KV_DIM), seq_a),
        pl.BlockSpec((r, LANES), lambda g, sc: (0, 0)),
        pl.BlockSpec((2 * LANES, 2 * LANES), lambda g, sc: (0, 0)),
        pl.BlockSpec((None, KV_DIM, LANES), seq_a),
        pl.BlockSpec((None, KV_DIM, LANES), seq_a),
        pl.BlockSpec(memory_space=pl.ANY),
        pl.BlockSpec(memory_space=pl.ANY),
    ]
    page_buf = pltpu.VMEM((2, n_pp, KV_DIM, PAGE_SIZE), F32)
    return pl.pallas_call(
        functools.partial(_sb_sample_kernel, n_pp=n_pp, dt=dt, steps=steps, n_seq=db, layer=layer),
        grid_spec=pltpu.PrefetchScalarGridSpec(
            num_scalar_prefetch=1,
            grid=(last + 1,),
            in_specs=in_specs,
            out_specs=pl.BlockSpec((None, r, KV_DIM), lambda g, sc: (jnp.maximum(g - 1, 0) // steps, 0, 0)),
            scratch_shapes=[pltpu.VMEM((r, LANES), F32), pltpu.VMEM((2, r, KV_DIM), F32),
                            pltpu.VMEM((r, n_pp * PAGE_SIZE), F32), pltpu.VMEM((r, n_pp * PAGE_SIZE), F32),
                            page_buf, page_buf,
                            pltpu.SemaphoreType.DMA((2,)), pltpu.SemaphoreType.DMA((2,))],
        ),
        out_shape=jax.ShapeDtypeStruct((db, r, KV_DIM), F32),
        compiler_params=_cparams(("arbitrary",)),
        name="sb_sample",
    )(sched, qe, bias_rows, _cumsum_total_matrix(), knew, vnew, cache_kt, cache_vt)


def _swa_prompt_kernel(sink_ref, q_ref, kvb_ref, o_ref, half_ref):
    qi = pl.program_id(1)
    rows = GROUP * Q_BLOCK
    lane = lax.broadcasted_iota(jnp.int32, (Q_BLOCK, LANES), 1)
    r_idx = lax.broadcasted_iota(jnp.int32, (rows, 2 * LANES), 0) & (Q_BLOCK - 1)
    c_idx = lax.broadcasted_iota(jnp.int32, (rows, 2 * LANES), 1)
    first_col = jnp.where(qi > 0, 0, WINDOW)
    mask = (c_idx >= r_idx) & (c_idx <= r_idx + WINDOW) & (c_idx >= first_col)
    prev = jnp.maximum(qi - 1, 0)
    ones = jnp.ones((LANES, 2 * LANES), BF16)
    for j in range(N_PAIR):
        kt = jnp.concatenate([kvb_ref[prev, LANES * j:LANES * (j + 1), :],
                              kvb_ref[qi, LANES * j:LANES * (j + 1), :]], axis=1)
        vt = jnp.concatenate([kvb_ref[prev, KV_DIM + LANES * j:KV_DIM + LANES * (j + 1), :],
                              kvb_ref[qi, KV_DIM + LANES * j:KV_DIM + LANES * (j + 1), :]], axis=1)
        vt_ones = jnp.concatenate([vt, ones], axis=0)
        for p in range(2):
            in_half = (lane // HEAD_DIM) == p
            qs = jnp.concatenate(
                [jnp.where(in_half, q_ref[:, LANES * (GROUP * j + g):LANES * (GROUP * j + g + 1)], 0)
                 for g in range(GROUP)], axis=0)
            sink = jnp.concatenate(
                [jnp.full((Q_BLOCK, LANES), sink_ref[2 * (GROUP * j + g) + p], F32) for g in range(GROUP)],
                axis=0)
            s = jnp.where(mask, jnp.dot(qs, kt, preferred_element_type=F32), -jnp.inf)
            m = jnp.maximum(jnp.max(s, axis=-1, keepdims=True), sink)
            e = jnp.exp(s - jnp.concatenate([m, m], axis=1)).astype(BF16)
            pv = lax.dot_general(e, vt_ones, (((1,), (1,)), ((), ())), preferred_element_type=F32)
            o = pv[:, :LANES] / (pv[:, LANES:] + jnp.exp(sink - m))
            if p == 0:
                half_ref[...] = o
            else:
                lane_r = lax.broadcasted_iota(jnp.int32, (rows, LANES), 1)
                o_pair = jnp.where((lane_r // HEAD_DIM) == 0, half_ref[...], o).astype(BF16)
                for g in range(GROUP):
                    o_ref[:, LANES * (GROUP * j + g):LANES * (GROUP * j + g + 1)] = \
                        o_pair[Q_BLOCK * g:Q_BLOCK * (g + 1)]


def _swa_prompt(q, kvb, sink_perm):
    b, t, _ = q.shape
    nq = t // Q_BLOCK
    nck = kvb.shape[1]
    return pl.pallas_call(
        _swa_prompt_kernel,
        grid_spec=pltpu.PrefetchScalarGridSpec(
            num_scalar_prefetch=1,
            grid=(b, nq),
            in_specs=[
                pl.BlockSpec((None, Q_BLOCK, Q_DIM), lambda bi, qi, s: (bi, qi, 0)),
                pl.BlockSpec((None, nck, 2 * KV_DIM, LANES), lambda bi, qi, s: (bi, 0, 0, 0)),
            ],
            out_specs=pl.BlockSpec((None, Q_BLOCK, Q_DIM), lambda bi, qi, s: (bi, qi, 0)),
            scratch_shapes=[pltpu.VMEM((GROUP * Q_BLOCK, LANES), F32)],
        ),
        out_shape=jax.ShapeDtypeStruct((b, t, Q_DIM), BF16),
        compiler_params=_cparams(("arbitrary", "arbitrary")),
        name="swa_prompt",
    )(sink_perm, q, kvb)


def _swa_sample_kernel(qe_ref, sink_ref, kb_ref, vb_ref, kn_ref, vn_ref, o_ref, ko_ref, vo_ref,
                       *, n_seq, dt, past_len):
    r = qe_ref.shape[1]
    wb = kb_ref.shape[2]
    t_row = lax.broadcasted_iota(jnp.int32, (r, wb), 0) % dt
    col = lax.broadcasted_iota(jnp.int32, (r, wb), 1)
    dist_old = wb + t_row - col
    mask_old = (dist_old <= WINDOW) & (col >= wb - past_len)
    mask_new = (col >= wb - dt) & (col - (wb - dt) <= t_row)
    lane_k = lax.broadcasted_iota(jnp.int32, (KV_DIM, wb), 1)
    sink = sink_ref[...][:, 0:1]
    for i in range(n_seq):
        qe = qe_ref[i]
        kb, vb, kn, vn = kb_ref[i], vb_ref[i], kn_ref[i], vn_ref[i]
        s_old = jnp.where(mask_old, jnp.dot(qe, kb.astype(BF16), preferred_element_type=F32), -jnp.inf)
        s_new = jnp.where(mask_new, jnp.dot(qe, kn.astype(BF16), preferred_element_type=F32), -jnp.inf)
        m = jnp.maximum(jnp.maximum(jnp.max(s_old, axis=-1, keepdims=True),
                                    jnp.max(s_new, axis=-1, keepdims=True)), sink)
        e_old = jnp.exp(s_old - m)
        e_new = jnp.exp(s_new - m)
        denom = (jnp.sum(e_old, axis=-1, keepdims=True) + jnp.sum(e_new, axis=-1, keepdims=True)
                 + jnp.exp(sink - m))
        nt = (((1,), (1,)), ((), ()))
        o = (lax.dot_general(e_old.astype(BF16), vb.astype(BF16), nt, preferred_element_type=F32)
             + lax.dot_general(e_new.astype(BF16), vn.astype(BF16), nt, preferred_element_type=F32))
        o_ref[i] = o / denom
        keep = lane_k < wb - dt
        ko_ref[i] = jnp.where(keep, pltpu.roll(kb, wb - dt, 1), kn)
        vo_ref[i] = jnp.where(keep, pltpu.roll(vb, wb - dt, 1), vn)


def _swa_sample(qe, sink_rows, buf_kt, buf_vt, layer, knew, vnew, dt, past_len):
    db, r, _ = qe.shape
    wb = buf_kt.shape[-1]
    n_seq = 4 if db % 4 == 0 else 1
    seq_spec = lambda shape: pl.BlockSpec((n_seq,) + shape, lambda b: (b, 0, 0))
    buf_spec = pl.BlockSpec((None, n_seq, KV_DIM, wb), lambda b: (layer, b, 0, 0))
    return pl.pallas_call(
        functools.partial(_swa_sample_kernel, n_seq=n_seq, dt=dt, past_len=past_len),
        grid=(db // n_seq,),
        in_specs=[seq_spec((r, KV_DIM)), pl.BlockSpec((r, LANES), lambda b: (0, 0)),
                  buf_spec, buf_spec, seq_spec((KV_DIM, wb)), seq_spec((KV_DIM, wb))],
        out_specs=[seq_spec((r, KV_DIM)), seq_spec((KV_DIM, wb)), seq_spec((KV_DIM, wb))],
        out_shape=[jax.ShapeDtypeStruct((db, r, KV_DIM), F32),
                   jax.ShapeDtypeStruct((db, KV_DIM, wb), F32),
                   jax.ShapeDtypeStruct((db, KV_DIM, wb), F32)],
        compiler_params=_cparams(("arbitrary",)),
        name="swa_sample",
    )(qe, sink_rows, buf_kt, buf_vt, knew, vnew)


_GELU_C0 = 0.7978845608028654
_GELU_C1 = 0.7978845608028654 * 0.044715


def _gelu_tanh(x):
    t = jnp.tanh(x * (_GELU_C0 + _GELU_C1 * (x * x)))
    return x * (0.5 + 0.5 * t)


def _ffn_kernel(*refs, halo, stride, multi_tile, rb, tf, resident):
    if resident:
        (o_ref, x_ref, gm_ref, sh_ref, sc_ref, gf_ref, gpm_ref, gpre_ref, gpost_ref, wo_ref,
         wup_ref, cw_ref, cb_ref, wd_ref, prev_ref, out_ref, tail_ref,
         x1_ref, h_ref, act_ref, buf0_ref, buf1_ref, carry_ref) = refs
    else:
        (o_ref, x_ref, gm_ref, sh_ref, sc_ref, gf_ref, gpm_ref, gpre_ref, gpost_ref, wo_ref,
         wg_ref, wv_ref, cwg_ref, cwv_ref, cbg_ref, cbv_ref, wd_ref, pg_ref, pv_ref, out_ref, tg_ref, tv_ref,
         x1_ref, h_ref, act_ref, buf0_ref, buf1_ref, carry_ref) = refs
    ti = pl.program_id(1)
    nc = act_ref.shape[0]
    dff = wd_ref.shape[0]
    tm = x_ref.shape[0]
    bufs = (buf0_ref, buf1_ref)

    def pair(gate_val):
        return jnp.concatenate(gate_val, axis=1)

    def chunk_cols(cc, full_ref, g_ref, v_ref):
        if resident:
            return pair([full_ref[:, tf * cc:tf * (cc + 1)], full_ref[:, dff + tf * cc:dff + tf * (cc + 1)]])
        return pair([g_ref[...], v_ref[...]])

    def project(cc, buf):
        prev = (chunk_cols(cc, prev_ref, None, None) if resident else chunk_cols(cc, None, pg_ref, pv_ref))
        if multi_tile:
            first = jnp.full((halo, 2 * tf), ti, jnp.int32) == 0
            prev = jnp.where(first, prev, carry_ref[cc])
        buf[0:halo, :] = prev
        w = chunk_cols(cc, wup_ref, None, None) if resident else chunk_cols(cc, None, wg_ref, wv_ref)
        buf[halo:halo + tm, :] = jnp.dot(h_ref[...], w, preferred_element_type=F32)
        last = buf[tm:tm + halo, :]
        if multi_tile:
            carry_ref[cc] = last
        if resident:
            tail_ref[:, tf * cc:tf * (cc + 1)] = last[:, :tf]
            tail_ref[:, dff + tf * cc:dff + tf * (cc + 1)] = last[:, tf:]
        else:
            tg_ref[...] = last[:, :tf]
            tv_ref[...] = last[:, tf:]

    def activate(cc, buf):
        cw = chunk_cols(cc, cw_ref, None, None) if resident else chunk_cols(cc, None, cwg_ref, cwv_ref)
        cb = chunk_cols(cc, cb_ref, None, None) if resident else chunk_cols(cc, None, cbg_ref, cbv_ref)
        for r0 in range(0, tm, rb):
            u0 = buf[halo + r0:halo + r0 + rb, :]
            u1 = buf[halo - stride + r0:halo - stride + r0 + rb, :]
            u2 = buf[halo - 2 * stride + r0:halo - 2 * stride + r0 + rb, :]
            conv = cb + cw[0:1] * u2 + cw[1:2] * u1 + cw[2:3] * u0
            act_ref[cc, r0:r0 + rb, :] = (_gelu_tanh(conv[:, :tf]) * conv[:, tf:]).astype(BF16)

    halves = [(0, tm)] if tm % 16 else [(0, tm // 2), (tm // 2, tm)]

    def mod_rows(ref, lo, hi):
        m = ref[...]
        return m if m.shape[0] == 1 else _rows(m, tm)[lo:hi]

    def prologue():
        ys = [jnp.dot(o_ref[lo:hi, :], wo_ref[...], preferred_element_type=F32) for lo, hi in halves]
        for (lo, hi), y in zip(halves, ys):
            x1 = x_ref[lo:hi, :] + mod_rows(gm_ref, lo, hi) * _rms(y, gpm_ref[...])
            x1_ref[lo:hi, :] = x1
            h_ref[lo:hi, :] = (_rms(x1, gpre_ref[...]) * (1.0 + mod_rows(sc_ref, lo, hi))
                               + mod_rows(sh_ref, lo, hi)).astype(BF16)

    def epilogue():
        kd = (nc - 1) * tf
        for lo, hi in halves:
            y = jnp.dot(act_ref[nc - 1, lo:hi, :], wd_ref[kd:, :], preferred_element_type=F32)
            if nc > 1:
                early = jnp.concatenate([act_ref[k, lo:hi, :] for k in range(nc - 1)], axis=1)
                y = y + jnp.dot(early, wd_ref[:kd, :], preferred_element_type=F32)
            out_ref[lo:hi, :] = x1_ref[lo:hi, :] + mod_rows(gf_ref, lo, hi) * _rms(y, gpost_ref[...])

    if resident:
        prologue()
        project(0, bufs[0])
        for cc in range(1, nc):
            activate(cc - 1, bufs[(cc - 1) % 2])
            project(cc, bufs[cc % 2])
        activate(nc - 1, bufs[(nc - 1) % 2])
        epilogue()
        return

    c = pl.program_id(2)

    @pl.when(c == 0)
    def _():
        prologue()
        project(c, bufs[0])

    for parity in range(2):
        @pl.when((c > 0) & (c < nc) & (lax.rem(c, 2) == parity))
        def _():
            activate(c - 1, bufs[1 - parity])
            project(c, bufs[parity])

    @pl.when(c == nc)
    def _():
        activate(c - 1, bufs[(nc - 1) % 2])
        epilogue()


def _ffn(o, x, gate_m, shift, scale, gate_f, gpost_m, gpre, gpost, wo, w_up, cw, cb, wd, prev,
         *, tm, tf, halo, stride, mod_map, resident):
    nb, r, d = x.shape
    dff = wd[0].shape[-2]
    nc = dff // tf
    nt = r // tm
    mr = shift[0].shape[-2]
    slabs = (gate_m, shift, scale, gate_f, gpost_m, gpre, gpost, wo, w_up, cw, cb, wd)
    gate_m_a, shift_a, scale_a, gate_f_a, gpost_m_a, gpre_a, gpost_a, wo_a, w_up_a, cw_a, cb_a, wd_a = (
        s[0] for s in slabs)
    multi_tile = nt > 1
    rb = 64 if tm % 64 == 0 else tm
    once = pl.Buffered(1)
    row_mode = dict(pipeline_mode=once) if nb * nt == 1 else {}
    kern = functools.partial(_ffn_kernel, halo=halo, stride=stride, multi_tile=multi_tile, rb=rb, tf=tf,
                             resident=resident)
    scratch = [
        pltpu.VMEM((tm, d), F32), pltpu.VMEM((tm, d), BF16), pltpu.VMEM((nc, tm, tf), BF16),
        pltpu.VMEM((halo + tm, 2 * tf), F32), pltpu.VMEM((halo + tm, 2 * tf), F32),
        pltpu.VMEM((nc, halo, 2 * tf) if multi_tile else (1, 8, LANES), F32),
    ]
    out_sds = jax.ShapeDtypeStruct((nb, r, d), F32)
    if resident:
        im = lambda b, t: (b, t, 0)
        seq = lambda b, t: (b, 0, 0)
        const = lambda b, t: (0, 0)
        mod_spec = lambda p: _slab(p, (None, mr, d), lambda b, t: (mod_map(b), 0, 0))
        vec_spec = lambda p: _slab(p, (1, d), const)
        whole = lambda p, **kw: _slab(p, p[0].shape[-2:], const, **kw)
        out, tail = pl.pallas_call(
            kern, grid=(nb, nt),
            in_specs=[
                pl.BlockSpec((None, tm, Q_DIM), im, **row_mode), pl.BlockSpec((None, tm, d), im, **row_mode),
                mod_spec(gate_m), mod_spec(shift), mod_spec(scale), mod_spec(gate_f),
                vec_spec(gpost_m), vec_spec(gpre), vec_spec(gpost),
                whole(wo, pipeline_mode=once), whole(w_up, pipeline_mode=once), whole(cw), whole(cb),
                whole(wd, pipeline_mode=once),
                pl.BlockSpec((None, halo, 2 * dff), seq),
            ],
            out_specs=[pl.BlockSpec((None, tm, d), im), pl.BlockSpec((None, halo, 2 * dff), seq)],
            out_shape=[out_sds, jax.ShapeDtypeStruct((nb, halo, 2 * dff), F32)],
            scratch_shapes=scratch,
            compiler_params=_cparams(("arbitrary", "arbitrary")),
            name="mixer_out_conv_ffn",
        )(o, x, gate_m_a, shift_a, scale_a, gate_f_a, gpost_m_a, gpre_a, gpost_a, wo_a, w_up_a, cw_a, cb_a,
          wd_a, prev)
        return out, tail

    assert not multi_tile
    last = nc - 1
    im = lambda b, t, c: (b, t, 0)
    const = lambda b, t, c: (0, 0)
    mod_spec = lambda p: _slab(p, (None, mr, d), lambda b, t, c: (mod_map(b), 0, 0))
    vec_spec = lambda p: _slab(p, (1, d), const)
    whole = lambda p, **kw: _slab(p, p[0].shape[-2:], const, **kw)
    proj = lambda p, off: _slab(p, (p[0].shape[-2], tf), lambda b, t, c: (0, off + jnp.minimum(c, last)))
    actv = lambda p, off: _slab(p, (p[0].shape[-2], tf), lambda b, t, c: (0, off + jnp.maximum(c - 1, 0)))
    halo_spec = lambda off: pl.BlockSpec((None, halo, tf), lambda b, t, c: (b, 0, off + jnp.minimum(c, last)))
    tail_sds = jax.ShapeDtypeStruct((nb, halo, dff), F32)
    out, tail_g, tail_v = pl.pallas_call(
        kern, grid=(nb, nt, nc + 1),
        in_specs=[
            pl.BlockSpec((None, tm, Q_DIM), im, **row_mode), pl.BlockSpec((None, tm, d), im, **row_mode),
            mod_spec(gate_m), mod_spec(shift), mod_spec(scale), mod_spec(gate_f),
            vec_spec(gpost_m), vec_spec(gpre), vec_spec(gpost),
            whole(wo, pipeline_mode=once),
            proj(w_up, 0), proj(w_up, nc), actv(cw, 0), actv(cw, nc), actv(cb, 0), actv(cb, nc),
            whole(wd, pipeline_mode=once),
            halo_spec(0), halo_spec(nc),
        ],
        out_specs=[pl.BlockSpec((None, tm, d), im), halo_spec(0), halo_spec(0)],
        out_shape=[out_sds, tail_sds, tail_sds],
        scratch_shapes=scratch,
        compiler_params=_cparams(("arbitrary", "arbitrary", "arbitrary")),
        name="mixer_out_conv_ffn_chunked",
    )(o, x, gate_m_a, shift_a, scale_a, gate_f_a, gpost_m_a, gpre_a, gpost_a, wo_a, w_up_a, w_up_a,
      cw_a, cw_a, cb_a, cb_a, wd_a, prev, prev)
    return out, jnp.concatenate([tail_g, tail_v], axis=-1)


def _rope_tables(pos):
    inv_freq = jnp.power(jnp.float32(ROPE_THETA), -jnp.arange(ROT_HALF, dtype=F32) * (2.0 / ROT_DIM))
    ang = pos.astype(F32)[:, None] * inv_freq[None, :]
    cos, sin = jnp.cos(ang), jnp.sin(ang)
    n = pos.shape[0]
    ones = jnp.ones((n, HEAD_DIM - ROT_DIM), F32)
    zeros_r = jnp.zeros((n, HEAD_DIM - ROT_DIM), F32)
    zeros_h = jnp.zeros((n, ROT_HALF), F32)
    cq = jnp.concatenate([cos, cos, ones], axis=1)
    s1 = jnp.concatenate([zeros_h, sin, zeros_r], axis=1)
    s2 = jnp.concatenate([-sin, zeros_h, zeros_r], axis=1)
    rep = LANES // HEAD_DIM
    return (jnp.tile(cq, (1, rep)), jnp.tile(s1, (1, rep)), jnp.tile(s2, (1, rep)), cos.T, sin.T)


def _kv_onehot():
    oh = np.zeros((N_QBLK, 2, N_KV_HEADS), np.float32)
    for i in range(N_QBLK):
        for p in range(2):
            oh[i, p, 2 * (i // GROUP) + p] = 1.0
    return oh


def _expand_q(q_tm):
    dt, db, _ = q_tm.shape
    q5 = q_tm.reshape(dt, db, N_QBLK, 2, HEAD_DIM).transpose(1, 2, 3, 0, 4)
    oh = jnp.asarray(_kv_onehot(), q_tm.dtype)
    q6 = q5[:, :, :, :, None, :] * oh[None, :, :, None, :, None]
    return q6.reshape(db, N_QBLK * 2 * dt, KV_DIM)


def _select_o(o_full, dt):
    db = o_full.shape[0]
    o6 = o_full.reshape(db, N_QBLK, 2, dt, N_KV_HEADS, HEAD_DIM)
    o5 = jnp.sum(o6 * jnp.asarray(_kv_onehot())[None, :, :, None, :, None], axis=4)
    return o5.transpose(3, 0, 1, 2, 4).reshape(dt, db, Q_DIM).astype(BF16)


def _rows_of_heads(v, dt):
    vp = v[jnp.asarray(HEAD_PERM)]
    return jnp.broadcast_to(vp[:, None, None], (N_HEADS, dt, LANES)).reshape(N_HEADS * dt, LANES).astype(F32)


def _from_t(kt, lead):
    n = kt.shape[-1]
    x = kt.reshape(lead + (N_KV_HEADS, HEAD_DIM, n))
    nd = len(lead)
    return x.transpose(tuple(range(nd)) + (nd + 2, nd, nd + 1))


def kernel(x_prompt, x_sample, cache_sb_k, cache_sb_v, state_swa_k, state_swa_v, state_conv, page_table,
           c_prompt, c_sample, mod_w, mod_b, norm_mix_pre, norm_mix_post, norm_ffn_pre, norm_ffn_post,
           sb_w_qkv, sb_w_o, sb_bias, swa_w_qkv, swa_w_o, swa_sinks, ffn_w_up, ffn_conv_w, ffn_conv_b,
           ffn_w_down):
    b, t, d = x_prompt.shape
    db, dt, _ = x_sample.shape
    depth = mod_w.shape[0]
    n_pages = page_table.shape[1]
    past_len = n_pages * PAGE_SIZE
    wb = state_swa_k.shape[2]
    dff = ffn_w_down.shape[1]
    n_phys = cache_sb_k.shape[1]

    tm_p = min(512, t)
    tm_f = min(512, t)
    tf = 256

    mod = _modulation(jnp.concatenate([c_prompt, c_sample], axis=0), mod_w, mod_b)
    mod_p = mod[:, :, :b].reshape(depth, N_MOD, b, 1, d)
    mod_s = mod[:, :, b:].reshape(depth, N_MOD, 1, db, d)

    cache_kt = cache_sb_k.transpose(0, 1, 3, 4, 2).reshape(-1, n_phys, KV_DIM, PAGE_SIZE)
    cache_vt = cache_sb_v.transpose(0, 1, 3, 4, 2).reshape(-1, n_phys, KV_DIM, PAGE_SIZE)
    swa_kt = state_swa_k.transpose(0, 1, 3, 4, 2).reshape(-1, db, KV_DIM, wb)
    swa_vt = state_swa_v.transpose(0, 1, 3, 4, 2).reshape(-1, db, KV_DIM, wb)

    qcols = jnp.asarray(np.concatenate([np.arange(HEAD_DIM) + HEAD_DIM * h for h in HEAD_PERM]))
    perm = jnp.asarray(HEAD_PERM)

    def prep_mixer(w_qkv, w_o):
        return (w_qkv[:, :, :Q_DIM][:, :, qcols].astype(BF16),
                w_qkv[:, :, Q_DIM:].transpose(0, 2, 1).astype(BF16),
                w_o[:, qcols, :].astype(BF16))
    mixer_w = (prep_mixer(sb_w_qkv, sb_w_o), prep_mixer(swa_w_qkv, swa_w_o))
    w_up_all = ffn_w_up.astype(BF16)
    wd_all = ffn_w_down.astype(BF16)
    cb_all = ffn_conv_b.reshape(depth, 1, 2 * dff)
    gains = [g.reshape(depth, 1, d) for g in (norm_mix_pre, norm_mix_post, norm_ffn_pre, norm_ffn_post)]

    rope_p = _rope_tables(jnp.arange(t))
    rope_p = tuple(a[None] for a in rope_p) + (False,)
    cq, s1, s2, ck, sk = _rope_tables(past_len + jnp.arange(dt))
    rope_s = (cq[:, None, :], s1[:, None, :], s2[:, None, :],
              jnp.broadcast_to(ck.T[:, :, None], (dt, ROT_HALF, db)),
              jnp.broadcast_to(sk.T[:, :, None], (dt, ROT_HALF, db)), True)

    xp = x_prompt
    xs = x_sample.transpose(1, 0, 2)
    pmap = lambda bi: bi
    smap = lambda bi: 0

    outs = {k: [] for k in ("sbkp", "sbvp", "sbks", "sbvs", "swkp", "swvp", "swks", "swvs", "cp", "cs")}
    for i in range(depth):
        j = i // 2
        is_sb = i % 2 == 0
        wq, wkvt, wo = ((w, (j,)) for w in mixer_w[0 if is_sb else 1])
        g_pre, g_post, gf_pre, gf_post = ((g, (i,)) for g in gains)
        mp = [(mod_p, (i, k)) for k in range(N_MOD)]
        ms = [(mod_s, (i, k)) for k in range(N_MOD)]

        qp, ktp, vtp, kvbp = _qkv(xp, mp[0], mp[1], g_pre, wq, wkvt, None if is_sb else rope_p,
                                  tm=tm_p, mod_map=pmap)
        qs, kts, vts, _ = _qkv(xs, ms[0], ms[1], g_pre, wq, wkvt, None if is_sb else rope_s,
                               tm=db, mod_map=smap)
        qe = _expand_q(qs)
        kn = kts.transpose(2, 1, 0)
        vn = vts.transpose(2, 1, 0)
        if is_sb:
            op = _sb_prompt(qp, kvbp, sb_bias[j][perm])
            pad = ((0, 0), (0, 0), (0, LANES - dt))
            o_full = _sb_sample(page_table, qe, _rows_of_heads(sb_bias[j], dt),
                                jnp.pad(kn, pad).astype(BF16), jnp.pad(vn, pad).astype(BF16),
                                cache_kt, cache_vt, j, dt)
            outs["sbkp"].append(_from_t(ktp, (b,)))
            outs["sbvp"].append(_from_t(vtp, (b,)))
            outs["sbks"].append(_from_t(kn, (db,)))
            outs["sbvs"].append(_from_t(vn, (db,)))
        else:
            op = _swa_prompt(qp, kvbp, swa_sinks[j][perm])
            pad = ((0, 0), (0, 0), (wb - dt, 0))
            o_full, ko, vo = _swa_sample(qe, _rows_of_heads(swa_sinks[j], dt), swa_kt, swa_vt, j,
                                         jnp.pad(kn, pad), jnp.pad(vn, pad), dt, past_len)
            keep = min(WINDOW, t)
            outs["swkp"].append(_from_t(ktp[:, :, t - keep:], (b,)))
            outs["swvp"].append(_from_t(vtp[:, :, t - keep:], (b,)))
            outs["swks"].append(_from_t(ko, (db,)))
            outs["swvs"].append(_from_t(vo, (db,)))
        os_ = _select_o(o_full, dt)

        w_up, wd, cw, cb = (w_up_all, (i,)), (wd_all, (i,)), (ffn_conv_w, (i,)), (cb_all, (i,))

        halo_p = 8
        zero_prev = jnp.zeros((b, halo_p, 2 * dff), F32)
        xp, tail_p = _ffn(op, xp, mp[2], mp[3], mp[4], mp[5], g_post, gf_pre, gf_post, wo,
                          w_up, cw, cb, wd, zero_prev,
                          tm=tm_f, tf=tf, halo=halo_p, stride=1, mod_map=pmap, resident=True)
        st = state_conv[i].transpose(1, 0, 2).reshape(1, 2 * db, 2 * dff)
        xs1, tail_s = _ffn(os_.reshape(1, dt * db, Q_DIM), xs.reshape(1, dt * db, d),
                           ms[2], ms[3], ms[4], ms[5], g_post, gf_pre, gf_post, wo,
                           w_up, cw, cb, wd, st,
                           tm=dt * db, tf=tf, halo=2 * db, stride=db, mod_map=smap, resident=False)
        xs = xs1.reshape(dt, db, d)
        outs["cp"].append(tail_p[:, halo_p - 2:])
        outs["cs"].append(tail_s.reshape(2, db, 2 * dff).transpose(1, 0, 2))

    st_ = lambda k: jnp.stack(outs[k])
    return (xp, xs.transpose(1, 0, 2),
            st_("sbkp"), st_("sbvp"), st_("sbks"), st_("sbvs"),
            st_("swkp"), st_("swvp"), st_("swks"), st_("swvs"),
            st_("cp"), st_("cs"))
```

```python
import functools

import numpy as np
import jax
import jax.numpy as jnp
from jax import lax
from jax.experimental import pallas as pl
from jax.experimental.pallas import tpu as pltpu

F32 = jnp.float32
BF16 = jnp.bfloat16

HEAD_DIM = 64
N_HEADS = 16
N_KV_HEADS = 4
GROUP = N_HEADS // N_KV_HEADS
Q_DIM = N_HEADS * HEAD_DIM
KV_DIM = N_KV_HEADS * HEAD_DIM
PAGE_SIZE = 128
WINDOW = 128
Q_BLOCK = 128
ROT_DIM = HEAD_DIM // 4
ROT_HALF = ROT_DIM // 2
ROPE_THETA = 500000.0
RMS_EPS = 1e-6
N_MOD = 6
LANES = 128
N_QBLK = Q_DIM // LANES
N_PAIR = KV_DIM // LANES
VMEM_LIMIT = 56 * 1024 * 1024

HEAD_PERM = [4 * (2 * (i // 4) + p) + (i % 4) for i in range(N_QBLK) for p in range(2)]


def _cparams(sem):
    return pltpu.CompilerParams(dimension_semantics=sem, vmem_limit_bytes=VMEM_LIMIT)


def _rms(x, g):
    return x * lax.rsqrt(jnp.mean(x * x, axis=-1, keepdims=True) + RMS_EPS) * g


def _rows(m, rows):
    mr = m.shape[0]
    if mr == 1 or mr == rows:
        return m
    return jnp.concatenate([m] * (rows // mr), axis=0)


def _norm_mod(x, g, shift, scale):
    rows = x.shape[0]
    return _rms(x, g) * (1.0 + _rows(scale, rows)) + _rows(shift, rows)


def _slab(param, tail_block, tail_index, **kw):
    _, lead = param
    lead = tuple(lead)
    return pl.BlockSpec((None,) * len(lead) + tuple(tail_block),
                        lambda *grid: lead + tuple(tail_index(*grid)), **kw)


def _mod_kernel(c_ref, w_ref, b_ref, o_ref):
    c = c_ref[...]
    s = c / (1.0 + jnp.exp(-c))
    o_ref[...] = jnp.dot(s.astype(BF16), w_ref[...].astype(BF16), preferred_element_type=F32) + b_ref[...]


def _modulation(c_all, mod_w, mod_b):
    depth, d, _ = mod_w.shape
    n = c_all.shape[0]
    return pl.pallas_call(
        _mod_kernel,
        grid=(depth, N_MOD),
        in_specs=[
            pl.BlockSpec((n, d), lambda i, k: (0, 0)),
            pl.BlockSpec((None, d, d), lambda i, k: (i, 0, k)),
            pl.BlockSpec((None, None, 1, d), lambda i, k: (i, k, 0, 0)),
        ],
        out_specs=pl.BlockSpec((None, None, n, d), lambda i, k: (i, k, 0, 0)),
        out_shape=jax.ShapeDtypeStruct((depth, N_MOD, n, d), F32),
        compiler_params=_cparams(("arbitrary", "arbitrary")),
        name="modulation",
    )(c_all, mod_w, mod_b.reshape(depth, N_MOD, 1, d))


def _qkv_kernel(*refs, rope, kc):
    if rope:
        (x_ref, sh_ref, sc_ref, g_ref, wq_ref, wkv_ref, cq_ref, s1_ref, s2_ref, ck_ref, sk_ref,
         q_ref, kt_ref, vt_ref, kvb_ref) = refs
    else:
        x_ref, sh_ref, sc_ref, g_ref, wq_ref, wkv_ref, q_ref, kt_ref, vt_ref, kvb_ref = refs
    tm = x_ref.shape[0]
    h = _norm_mod(x_ref[...], g_ref[...], sh_ref[...], sc_ref[...]).astype(BF16)
    q = jnp.dot(h, wq_ref[...], preferred_element_type=F32)
    kvt = lax.dot_general(wkv_ref[...], h, (((1,), (1,)), ((), ())),
                          preferred_element_type=F32)
    scale = HEAD_DIM ** -0.5
    if rope:
        cq, s1, s2 = cq_ref[...], s1_ref[...], s2_ref[...]
        for i in range(N_QBLK):
            xr = q[:, LANES * i:LANES * (i + 1)]
            rot = xr * cq + pltpu.roll(xr, ROT_HALF, 1) * s1 + pltpu.roll(xr, LANES - ROT_HALF, 1) * s2
            q_ref[:, LANES * i:LANES * (i + 1)] = (rot * scale).astype(BF16)
        ck, sk = ck_ref[...], sk_ref[...]
        pieces = []
        for c in range(N_KV_HEADS):
            base = HEAD_DIM * c
            x1 = kvt[base:base + ROT_HALF]
            x2 = kvt[base + ROT_HALF:base + ROT_DIM]
            pieces += [x1 * ck - x2 * sk, x2 * ck + x1 * sk, kvt[base + ROT_DIM:base + HEAD_DIM]]
        kt = jnp.concatenate(pieces, axis=0)
    else:
        q_ref[...] = (q * scale).astype(BF16)
        kt = kvt[:KV_DIM]
    vt = kvt[KV_DIM:]
    kt_ref[...] = kt
    vt_ref[...] = vt
    kb = kt.astype(BF16)
    vb = vt.astype(BF16)
    for c in range(tm // kc):
        kvb_ref[c, :KV_DIM, :] = kb[:, kc * c:kc * (c + 1)]
        kvb_ref[c, KV_DIM:, :] = vb[:, kc * c:kc * (c + 1)]


def _qkv(x, shift, scale, g, wq, wkvt, rope_tabs, *, tm, mod_map):
    nb, r, d = x.shape
    kc = min(LANES, tm)
    nt = r // tm
    rope = rope_tabs is not None
    mr = shift[0].shape[-2]
    const = lambda b, t: (0, 0)
    mod_index = lambda b, t: (mod_map(b), 0, 0)
    in_specs = [
        pl.BlockSpec((None, tm, d), lambda b, t: (b, t, 0)),
        _slab(shift, (None, mr, d), mod_index),
        _slab(scale, (None, mr, d), mod_index),
        _slab(g, (1, d), const),
        _slab(wq, wq[0].shape[-2:], const),
        _slab(wkvt, wkvt[0].shape[-2:], const),
    ]
    args = [x, shift[0], scale[0], g[0], wq[0], wkvt[0]]
    if rope:
        cq, s1, s2, ck, sk, per_b = rope_tabs
        if per_b:
            qspec = pl.BlockSpec((None, 1, LANES), lambda b, t: (b, 0, 0))
            kspec = pl.BlockSpec((None, ROT_HALF, tm), lambda b, t: (b, 0, 0))
        else:
            qspec = pl.BlockSpec((None, tm, LANES), lambda b, t: (0, t, 0))
            kspec = pl.BlockSpec((None, ROT_HALF, tm), lambda b, t: (0, 0, t))
        in_specs += [qspec, qspec, qspec, kspec, kspec]
        args += [cq, s1, s2, ck, sk]
    out_shape = [
        jax.ShapeDtypeStruct((nb, r, Q_DIM), BF16),
        jax.ShapeDtypeStruct((nb, KV_DIM, r), F32),
        jax.ShapeDtypeStruct((nb, KV_DIM, r), F32),
        jax.ShapeDtypeStruct((nb, r // kc, 2 * KV_DIM, kc), BF16),
    ]
    out_specs = [
        pl.BlockSpec((None, tm, Q_DIM), lambda b, t: (b, t, 0)),
        pl.BlockSpec((None, KV_DIM, tm), lambda b, t: (b, 0, t)),
        pl.BlockSpec((None, KV_DIM, tm), lambda b, t: (b, 0, t)),
        pl.BlockSpec((None, tm // kc, 2 * KV_DIM, kc), lambda b, t: (b, t, 0, 0)),
    ]
    return pl.pallas_call(
        functools.partial(_qkv_kernel, rope=rope, kc=kc),
        grid=(nb, nt),
        in_specs=in_specs,
        out_specs=out_specs,
        out_shape=out_shape,
        compiler_params=_cparams(("arbitrary", "arbitrary")),
        name="qkv_rope" if rope else "qkv",
    )(*args)


LOG2E = 1.4426950408889634


def _sb_split(z, mask):
    n = z.shape[1] // LANES
    sp = jnp.maximum(z, 0.0) + jnp.log(1.0 + jnp.exp2(jnp.abs(z) * (-LOG2E)))
    if mask is not None:
        sp = jnp.where(mask, sp, 0.0)
    hi_f = lax.bitcast_convert_type(
        lax.bitcast_convert_type(sp, jnp.uint32) & jnp.uint32(0xFFFF0000), F32)
    hi = hi_f.astype(BF16)
    lo = (sp - hi_f).astype(BF16)
    return jnp.concatenate(
        [jnp.concatenate([hi[:, LANES * k:LANES * (k + 1)], lo[:, LANES * k:LANES * (k + 1)]], axis=1)
         for k in range(n)], axis=0)


def _sb_log_weights(z, t, carry):
    r = z.shape[0]
    n = z.shape[1] // LANES
    with_totals = t.shape[1] == 2 * LANES
    tails = [None] * n
    for k in reversed(range(n)):
        blk = t[r * k:r * (k + 1)]
        tails[k] = blk[:, :LANES] + carry
        if with_totals:
            carry = carry + blk[:, LANES:]
        else:
            carry = jnp.broadcast_to(tails[k][:, 0:1], carry.shape)
    tail = tails[0] if n == 1 else jnp.concatenate(tails, axis=1)
    return z + tail, carry


def _sb_finish(z, t, mask, carry):
    lw, carry = _sb_log_weights(z, t, carry)
    a = jnp.exp(lw)
    if mask is not None:
        a = jnp.where(mask, a, 0.0)
    return a.astype(BF16), carry


def _cumsum_total_matrix():
    j = np.arange(LANES)[:, None]
    c = np.arange(LANES)[None, :]
    half = np.concatenate([-(j >= c).astype(np.float32), -np.ones((LANES, LANES), np.float32)], axis=1)
    return jnp.asarray(np.concatenate([half, half], axis=0), dtype=BF16)


def _sb_prompt_kernel(bias_ref, q_ref, kvb_ref, w_ref, o_ref, *scratch):
    qi = pl.program_id(1)
    rows = GROUP * Q_BLOCK
    lane = lax.broadcasted_iota(jnp.int32, (Q_BLOCK, LANES), 1)
    top = qi // 2
    qs_refs, c_refs, acc_refs = scratch[0:4], scratch[4:8], scratch[8:12]
    chains = [(j, p) for j in range(N_PAIR) for p in range(2)]
    nt_dims = (((1,), (1,)), ((), ()))
    for n, (j, p) in enumerate(chains):
        in_half = (lane // HEAD_DIM) == p
        for g in range(GROUP):
            blk = q_ref[:, LANES * (GROUP * j + g):LANES * (GROUP * j + g + 1)]
            qs_refs[n][Q_BLOCK * g:Q_BLOCK * (g + 1), :] = jnp.where(in_half, blk, 0)
        c_refs[n][...] = jnp.zeros_like(c_refs[n])
        acc_refs[n][...] = jnp.zeros_like(acc_refs[n])

    def chunk(ci, masked):
        kts = [jnp.concatenate([kvb_ref[2 * ci, LANES * j:LANES * (j + 1), :],
                                kvb_ref[2 * ci + 1, LANES * j:LANES * (j + 1), :]], axis=1)
               for j in range(N_PAIR)]
        vts = [jnp.concatenate([kvb_ref[2 * ci, KV_DIM + LANES * j:KV_DIM + LANES * (j + 1), :],
                                kvb_ref[2 * ci + 1, KV_DIM + LANES * j:KV_DIM + LANES * (j + 1), :]], axis=1)
               for j in range(N_PAIR)]
        mask = None
        if masked:
            r_idx = lax.broadcasted_iota(jnp.int32, (rows, 2 * LANES), 0) & (Q_BLOCK - 1)
            c_idx = lax.broadcasted_iota(jnp.int32, (rows, 2 * LANES), 1)
            mask = (ci * 2 * LANES + c_idx) < (qi * Q_BLOCK + r_idx)
        zs = [jnp.dot(qs_refs[n][...], kts[j], preferred_element_type=F32) for n, (j, p) in enumerate(chains)]
        zs = [jnp.concatenate(
            [zs[n][Q_BLOCK * g:Q_BLOCK * (g + 1)] + bias_ref[2 * (GROUP * j + g) + p]
             for g in range(GROUP)], axis=0) for n, (j, p) in enumerate(chains)]
        hls = [_sb_split(z, mask) for z in zs]
        ts = [jnp.dot(hl, w_ref[...], preferred_element_type=F32) for hl in hls]
        aw = [_sb_finish(zs[n], ts[n], mask, c_refs[n][...]) for n in range(len(chains))]
        for n in range(len(chains)):
            c_refs[n][...] = aw[n][1]
        pv = [lax.dot_general(aw[n][0], vts[j], nt_dims, preferred_element_type=F32)
              for n, (j, p) in enumerate(chains)]
        for n in range(len(chains)):
            acc_refs[n][...] += pv[n]

    chunk(top, True)

    def body(n, _):
        chunk(top - 1 - n, False)
        return 0

    lax.fori_loop(0, top, body, 0)
    lane_r = lax.broadcasted_iota(jnp.int32, (rows, LANES), 1)
    for j in range(N_PAIR):
        o_pair = jnp.where((lane_r // HEAD_DIM) == 0, acc_refs[2 * j][...],
                           acc_refs[2 * j + 1][...]).astype(BF16)
        for g in range(GROUP):
            o_ref[:, LANES * (GROUP * j + g):LANES * (GROUP * j + g + 1)] = \
                o_pair[Q_BLOCK * g:Q_BLOCK * (g + 1)]


def _sb_prompt(q, kvb, bias_perm):
    b, t, _ = q.shape
    nq = t // Q_BLOCK
    nck = kvb.shape[1]
    rows = GROUP * Q_BLOCK
    return pl.pallas_call(
        _sb_prompt_kernel,
        grid_spec=pltpu.PrefetchScalarGridSpec(
            num_scalar_prefetch=1,
            grid=(b, nq),
            in_specs=[
                pl.BlockSpec((None, Q_BLOCK, Q_DIM), lambda bi, qi, s: (bi, qi, 0)),
                pl.BlockSpec((None, nck, 2 * KV_DIM, LANES), lambda bi, qi, s: (bi, 0, 0, 0)),
                pl.BlockSpec((2 * LANES, LANES), lambda bi, qi, s: (0, 0)),
            ],
            out_specs=pl.BlockSpec((None, Q_BLOCK, Q_DIM), lambda bi, qi, s: (bi, qi, 0)),
            scratch_shapes=([pltpu.VMEM((rows, LANES), BF16)] * N_KV_HEADS
                            + [pltpu.VMEM((rows, LANES), F32)] * (2 * N_KV_HEADS)),
        ),
        out_shape=jax.ShapeDtypeStruct((b, t, Q_DIM), BF16),
        compiler_params=_cparams(("arbitrary", "arbitrary")),
        name="sb_prompt",
    )(bias_perm, q, kvb, _cumsum_total_matrix()[:, :LANES])


def _sb_sample_kernel(*refs, n_pp, dt, steps, n_seq, layer):
    (sched_ref, qe_ref, bias_ref, w_ref, kn_ref, vn_ref, ck_ref, cv_ref,
     o_ref, c_ref, acc_ref, lw0_ref, lw1_ref, kbuf_ref, vbuf_ref, ksem, vsem) = refs
    lws = (lw0_ref, lw1_ref)
    g = pl.program_id(0)
    last = n_seq * steps
    seq_a = g // steps
    seq_b = (g - 1) // steps
    r = qe_ref.shape[0]
    nt_dims = (((1,), (1,)), ((), ()))
    halves = 2 if n_pp % 2 == 0 else 1

    def page_copy(cache_ref, buf_ref, sem, page, slot, k):
        return pltpu.make_async_copy(cache_ref.at[layer, page], buf_ref.at[slot, k], sem.at[slot])

    def start_pages(cache_ref, buf_ref, sem, group, slot):
        for k in range(n_pp):
            page_copy(cache_ref, buf_ref, sem, sched_ref[group * n_pp + k], slot, k).start()

    def wait_pages(cache_ref, buf_ref, sem, slot):
        for k in range(n_pp):
            page_copy(cache_ref, buf_ref, sem, 0, slot, k).wait()

    def logits(slot):
        qe, bias = qe_ref[...], bias_ref[...]
        return jnp.concatenate(
            [jnp.dot(qe, kbuf_ref[slot, k].astype(BF16), preferred_element_type=F32) + bias
             for k in range(n_pp)], axis=1)

    def log_weights(z, lw_ref):
        hl = _sb_split(z, None)
        hr = hl.shape[0] // halves
        t = jnp.concatenate([jnp.dot(hl[i:i + hr], w_ref[...], preferred_element_type=F32)
                             for i in range(0, hl.shape[0], hr)], axis=0)
        lw, c_new = _sb_log_weights(z, t, c_ref[...])
        c_ref[...] = c_new
        lw_ref[...] = lw

    def weights_v(a, slot):
        hk = n_pp // halves
        pv = None
        for k0 in range(0, n_pp, hk):
            vt = jnp.concatenate([vbuf_ref[slot, k].astype(BF16) for k in range(k0, k0 + hk)], axis=1)
            part = lax.dot_general(a[:, PAGE_SIZE * k0:PAGE_SIZE * (k0 + hk)], vt, nt_dims,
                                   preferred_element_type=F32)
            pv = part if pv is None else pv + part
        return pv

    def accumulate(pv):
        slot = lax.rem(seq_b, 2)
        acc = acc_ref[slot] + pv
        acc_ref[slot] = acc
        o_ref[...] = acc

    @pl.when((lax.rem(g, steps) == 0) & (g < last))
    def _():
        t_row = lax.broadcasted_iota(jnp.int32, (r, LANES), 0) % dt
        t_key = lax.broadcasted_iota(jnp.int32, (r, LANES), 1)
        z = jnp.dot(qe_ref[...], kn_ref[...], preferred_element_type=F32) + bias_ref[...]
        mask = t_key < t_row
        t = jnp.dot(_sb_split(z, mask), w_ref[...], preferred_element_type=F32)
        a, c_new = _sb_finish(z, t, mask, jnp.zeros((r, LANES), F32))
        c_ref[...] = c_new
        acc_ref[lax.rem(seq_a, 2)] = lax.dot_general(a, vn_ref[...], nt_dims, preferred_element_type=F32)

    @pl.when(g == 0)
    def _():
        start_pages(ck_ref, kbuf_ref, ksem, 0, 0)
        start_pages(ck_ref, kbuf_ref, ksem, min(1, last - 1), 1)
        start_pages(cv_ref, vbuf_ref, vsem, 0, 0)
        wait_pages(ck_ref, kbuf_ref, ksem, 0)
        log_weights(logits(0), lws[0])

    for parity in range(2):
        @pl.when((g > 0) & (g < last) & (lax.rem(g, 2) == parity))
        def _():
            start_pages(ck_ref, kbuf_ref, ksem, jnp.minimum(g + 1, last - 1), 1 - parity)
            start_pages(cv_ref, vbuf_ref, vsem, g, parity)
            wait_pages(ck_ref, kbuf_ref, ksem, parity)
            wait_pages(cv_ref, vbuf_ref, vsem, 1 - parity)
            a = jnp.exp(lws[1 - parity][...]).astype(BF16)
            z = logits(parity)
            pv = weights_v(a, 1 - parity)
            log_weights(z, lws[parity])
            accumulate(pv)

    @pl.when(g == last)
    def _():
        wait_pages(ck_ref, kbuf_ref, ksem, last % 2)
        wait_pages(cv_ref, vbuf_ref, vsem, (last - 1) % 2)
        accumulate(weights_v(jnp.exp(lws[(last - 1) % 2][...]).astype(BF16), (last - 1) % 2))


def _sb_sample(page_table, qe, bias_rows, knew, vnew, cache_kt, cache_vt, layer, dt):
    db, r, _ = qe.shape
    n_pages = page_table.shape[1]
    n_pp = min(32, n_pages)
    while n_pages % n_pp:
        n_pp -= 1
    steps = n_pages // n_pp
    last = db * steps
    sched = page_table.reshape(db, steps, n_pp)[:, ::-1, :].reshape(last * n_pp)

    seq_a = lambda g, sc: (jnp.minimum(g // steps, db - 1), 0, 0)
    in_specs = [
        pl.BlockSpec((None, r, KV_DIM), seq_a),
        pl.BlockSpec((r, LANES), lambda g, sc: (0, 0)),
        pl.BlockSpec((2 * LANES, 2 * LANES), lambda g, sc: (0, 0)),
        pl.BlockSpec((None, KV_DIM, LANES), seq_a),
        pl.BlockSpec((None, KV_DIM, LANES), seq_a),
        pl.BlockSpec(memory_space=pl.ANY),
        pl.BlockSpec(memory_space=pl.ANY),
    ]
    page_buf = pltpu.VMEM((2, n_pp, KV_DIM, PAGE_SIZE), F32)
    return pl.pallas_call(
        functools.partial(_sb_sample_kernel, n_pp=n_pp, dt=dt, steps=steps, n_seq=db, layer=layer),
        grid_spec=pltpu.PrefetchScalarGridSpec(
            num_scalar_prefetch=1,
            grid=(last + 1,),
            in_specs=in_specs,
            out_specs=pl.BlockSpec((None, r, KV_DIM), lambda g, sc: (jnp.maximum(g - 1, 0) // steps, 0, 0)),
            scratch_shapes=[pltpu.VMEM((r, LANES), F32), pltpu.VMEM((2, r, KV_DIM), F32),
                            pltpu.VMEM((r, n_pp * PAGE_SIZE), F32), pltpu.VMEM((r, n_pp * PAGE_SIZE), F32),
                            page_buf, page_buf,
                            pltpu.SemaphoreType.DMA((2,)), pltpu.SemaphoreType.DMA((2,))],
        ),
        out_shape=jax.ShapeDtypeStruct((db, r, KV_DIM), F32),
        compiler_params=_cparams(("arbitrary",)),
        name="sb_sample",
    )(sched, qe, bias_rows, _cumsum_total_matrix(), knew, vnew, cache_kt, cache_vt)


def _swa_prompt_kernel(sink_ref, q_ref, kvb_ref, o_ref, half_ref):
    qi = pl.program_id(1)
    rows = GROUP * Q_BLOCK
    lane = lax.broadcasted_iota(jnp.int32, (Q_BLOCK, LANES), 1)
    r_idx = lax.broadcasted_iota(jnp.int32, (rows, 2 * LANES), 0) & (Q_BLOCK - 1)
    c_idx = lax.broadcasted_iota(jnp.int32, (rows, 2 * LANES), 1)
    first_col = jnp.where(qi > 0, 0, WINDOW)
    mask = (c_idx >= r_idx) & (c_idx <= r_idx + WINDOW) & (c_idx >= first_col)
    prev = jnp.maximum(qi - 1, 0)
    ones = jnp.ones((LANES, 2 * LANES), BF16)
    for j in range(N_PAIR):
        kt = jnp.concatenate([kvb_ref[prev, LANES * j:LANES * (j + 1), :],
                              kvb_ref[qi, LANES * j:LANES * (j + 1), :]], axis=1)
        vt = jnp.concatenate([kvb_ref[prev, KV_DIM + LANES * j:KV_DIM + LANES * (j + 1), :],
                              kvb_ref[qi, KV_DIM + LANES * j:KV_DIM + LANES * (j + 1), :]], axis=1)
        vt_ones = jnp.concatenate([vt, ones], axis=0)
        for p in range(2):
            in_half = (lane // HEAD_DIM) == p
            qs = jnp.concatenate(
                [jnp.where(in_half, q_ref[:, LANES * (GROUP * j + g):LANES * (GROUP * j + g + 1)], 0)
                 for g in range(GROUP)], axis=0)
            sink = jnp.concatenate(
                [jnp.full((Q_BLOCK, LANES), sink_ref[2 * (GROUP * j + g) + p], F32) for g in range(GROUP)],
                axis=0)
            s = jnp.where(mask, jnp.dot(qs, kt, preferred_element_type=F32), -jnp.inf)
            m = jnp.maximum(jnp.max(s, axis=-1, keepdims=True), sink)
            e = jnp.exp(s - jnp.concatenate([m, m], axis=1)).astype(BF16)
            pv = lax.dot_general(e, vt_ones, (((1,), (1,)), ((), ())), preferred_element_type=F32)
            o = pv[:, :LANES] / (pv[:, LANES:] + jnp.exp(sink - m))
            if p == 0:
                half_ref[...] = o
            else:
                lane_r = lax.broadcasted_iota(jnp.int32, (rows, LANES), 1)
                o_pair = jnp.where((lane_r // HEAD_DIM) == 0, half_ref[...], o).astype(BF16)
                for g in range(GROUP):
                    o_ref[:, LANES * (GROUP * j + g):LANES * (GROUP * j + g + 1)] = \
                        o_pair[Q_BLOCK * g:Q_BLOCK * (g + 1)]


def _swa_prompt(q, kvb, sink_perm):
    b, t, _ = q.shape
    nq = t // Q_BLOCK
    nck = kvb.shape[1]
    return pl.pallas_call(
        _swa_prompt_kernel,
        grid_spec=pltpu.PrefetchScalarGridSpec(
            num_scalar_prefetch=1,
            grid=(b, nq),
            in_specs=[
                pl.BlockSpec((None, Q_BLOCK, Q_DIM), lambda bi, qi, s: (bi, qi, 0)),
                pl.BlockSpec((None, nck, 2 * KV_DIM, LANES), lambda bi, qi, s: (bi, 0, 0, 0)),
            ],
            out_specs=pl.BlockSpec((None, Q_BLOCK, Q_DIM), lambda bi, qi, s: (bi, qi, 0)),
            scratch_shapes=[pltpu.VMEM((GROUP * Q_BLOCK, LANES), F32)],
        ),
        out_shape=jax.ShapeDtypeStruct((b, t, Q_DIM), BF16),
        compiler_params=_cparams(("arbitrary", "arbitrary")),
        name="swa_prompt",
    )(sink_perm, q, kvb)


def _swa_sample_kernel(qe_ref, sink_ref, kb_ref, vb_ref, kn_ref, vn_ref, o_ref, ko_ref, vo_ref,
                       *, n_seq, dt, past_len):
    r = qe_ref.shape[1]
    wb = kb_ref.shape[2]
    t_row = lax.broadcasted_iota(jnp.int32, (r, wb), 0) % dt
    col = lax.broadcasted_iota(jnp.int32, (r, wb), 1)
    dist_old = wb + t_row - col
    mask_old = (dist_old <= WINDOW) & (col >= wb - past_len)
    mask_new = (col >= wb - dt) & (col - (wb - dt) <= t_row)
    lane_k = lax.broadcasted_iota(jnp.int32, (KV_DIM, wb), 1)
    sink = sink_ref[...][:, 0:1]
    for i in range(n_seq):
        qe = qe_ref[i]
        kb, vb, kn, vn = kb_ref[i], vb_ref[i], kn_ref[i], vn_ref[i]
        s_old = jnp.where(mask_old, jnp.dot(qe, kb.astype(BF16), preferred_element_type=F32), -jnp.inf)
        s_new = jnp.where(mask_new, jnp.dot(qe, kn.astype(BF16), preferred_element_type=F32), -jnp.inf)
        m = jnp.maximum(jnp.maximum(jnp.max(s_old, axis=-1, keepdims=True),
                                    jnp.max(s_new, axis=-1, keepdims=True)), sink)
        e_old = jnp.exp(s_old - m)
        e_new = jnp.exp(s_new - m)
        denom = (jnp.sum(e_old, axis=-1, keepdims=True) + jnp.sum(e_new, axis=-1, keepdims=True)
                 + jnp.exp(sink - m))
        nt = (((1,), (1,)), ((), ()))
        o = (lax.dot_general(e_old.astype(BF16), vb.astype(BF16), nt, preferred_element_type=F32)
             + lax.dot_general(e_new.astype(BF16), vn.astype(BF16), nt, preferred_element_type=F32))
        o_ref[i] = o / denom
        keep = lane_k < wb - dt
        ko_ref[i] = jnp.where(keep, pltpu.roll(kb, wb - dt, 1), kn)
        vo_ref[i] = jnp.where(keep, pltpu.roll(vb, wb - dt, 1), vn)


def _swa_sample(qe, sink_rows, buf_kt, buf_vt, layer, knew, vnew, dt, past_len):
    db, r, _ = qe.shape
    wb = buf_kt.shape[-1]
    n_seq = 4 if db % 4 == 0 else 1
    seq_spec = lambda shape: pl.BlockSpec((n_seq,) + shape, lambda b: (b, 0, 0))
    buf_spec = pl.BlockSpec((None, n_seq, KV_DIM, wb), lambda b: (layer, b, 0, 0))
    return pl.pallas_call(
        functools.partial(_swa_sample_kernel, n_seq=n_seq, dt=dt, past_len=past_len),
        grid=(db // n_seq,),
        in_specs=[seq_spec((r, KV_DIM)), pl.BlockSpec((r, LANES), lambda b: (0, 0)),
                  buf_spec, buf_spec, seq_spec((KV_DIM, wb)), seq_spec((KV_DIM, wb))],
        out_specs=[seq_spec((r, KV_DIM)), seq_spec((KV_DIM, wb)), seq_spec((KV_DIM, wb))],
        out_shape=[jax.ShapeDtypeStruct((db, r, KV_DIM), F32),
                   jax.ShapeDtypeStruct((db, KV_DIM, wb), F32),
                   jax.ShapeDtypeStruct((db, KV_DIM, wb), F32)],
        compiler_params=_cparams(("arbitrary",)),
        name="swa_sample",
    )(qe, sink_rows, buf_kt, buf_vt, knew, vnew)


_GELU_C0 = 0.7978845608028654
_GELU_C1 = 0.7978845608028654 * 0.044715


def _gelu_tanh(x):
    t = jnp.tanh(x * (_GELU_C0 + _GELU_C1 * (x * x)))
    return x * (0.5 + 0.5 * t)


def _ffn_kernel(*refs, halo, stride, multi_tile, rb, tf, resident):
    if resident:
        (o_ref, x_ref, gm_ref, sh_ref, sc_ref, gf_ref, gpm_ref, gpre_ref, gpost_ref, wo_ref,
         wup_ref, cw_ref, cb_ref, wd_ref, prev_ref, out_ref, tail_ref,
         x1_ref, h_ref, act_ref, buf0_ref, buf1_ref, carry_ref) = refs
    else:
        (o_ref, x_ref, gm_ref, sh_ref, sc_ref, gf_ref, gpm_ref, gpre_ref, gpost_ref, wo_ref,
         wg_ref, wv_ref, cwg_ref, cwv_ref, cbg_ref, cbv_ref, wd_ref, pg_ref, pv_ref, out_ref, tg_ref, tv_ref,
         x1_ref, h_ref, act_ref, buf0_ref, buf1_ref, carry_ref) = refs
    ti = pl.program_id(1)
    nc = act_ref.shape[0]
    dff = wd_ref.shape[0]
    tm = x_ref.shape[0]
    bufs = (buf0_ref, buf1_ref)

    def pair(gate_val):
        return jnp.concatenate(gate_val, axis=1)

    def chunk_cols(cc, full_ref, g_ref, v_ref):
        if resident:
            return pair([full_ref[:, tf * cc:tf * (cc + 1)], full_ref[:, dff + tf * cc:dff + tf * (cc + 1)]])
        return pair([g_ref[...], v_ref[...]])

    def project(cc, buf):
        prev = (chunk_cols(cc, prev_ref, None, None) if resident else chunk_cols(cc, None, pg_ref, pv_ref))
        if multi_tile:
            first = jnp.full((halo, 2 * tf), ti, jnp.int32) == 0
            prev = jnp.where(first, prev, carry_ref[cc])
        buf[0:halo, :] = prev
        w = chunk_cols(cc, wup_ref, None, None) if resident else chunk_cols(cc, None, wg_ref, wv_ref)
        buf[halo:halo + tm, :] = jnp.dot(h_ref[...], w, preferred_element_type=F32)
        last = buf[tm:tm + halo, :]
        if multi_tile:
            carry_ref[cc] = last
        if resident:
            tail_ref[:, tf * cc:tf * (cc + 1)] = last[:, :tf]
            tail_ref[:, dff + tf * cc:dff + tf * (cc + 1)] = last[:, tf:]
        else:
            tg_ref[...] = last[:, :tf]
            tv_ref[...] = last[:, tf:]

    def activate(cc, buf):
        cw = chunk_cols(cc, cw_ref, None, None) if resident else chunk_cols(cc, None, cwg_ref, cwv_ref)
        cb = chunk_cols(cc, cb_ref, None, None) if resident else chunk_cols(cc, None, cbg_ref, cbv_ref)
        for r0 in range(0, tm, rb):
            u0 = buf[halo + r0:halo + r0 + rb, :]
            u1 = buf[halo - stride + r0:halo - stride + r0 + rb, :]
            u2 = buf[halo - 2 * stride + r0:halo - 2 * stride + r0 + rb, :]
            conv = cb + cw[0:1] * u2 + cw[1:2] * u1 + cw[2:3] * u0
            act_ref[cc, r0:r0 + rb, :] = (_gelu_tanh(conv[:, :tf]) * conv[:, tf:]).astype(BF16)

    halves = [(0, tm)] if tm % 16 else [(0, tm // 2), (tm // 2, tm)]

    def mod_rows(ref, lo, hi):
        m = ref[...]
        return m if m.shape[0] == 1 else _rows(m, tm)[lo:hi]

    def prologue():
        ys = [jnp.dot(o_ref[lo:hi, :], wo_ref[...], preferred_element_type=F32) for lo, hi in halves]
        for (lo, hi), y in zip(halves, ys):
            x1 = x_ref[lo:hi, :] + mod_rows(gm_ref, lo, hi) * _rms(y, gpm_ref[...])
            x1_ref[lo:hi, :] = x1
            h_ref[lo:hi, :] = (_rms(x1, gpre_ref[...]) * (1.0 + mod_rows(sc_ref, lo, hi))
                               + mod_rows(sh_ref, lo, hi)).astype(BF16)

    def epilogue():
        kd = (nc - 1) * tf
        for lo, hi in halves:
            y = jnp.dot(act_ref[nc - 1, lo:hi, :], wd_ref[kd:, :], preferred_element_type=F32)
            if nc > 1:
                early = jnp.concatenate([act_ref[k, lo:hi, :] for k in range(nc - 1)], axis=1)
                y = y + jnp.dot(early, wd_ref[:kd, :], preferred_element_type=F32)
            out_ref[lo:hi, :] = x1_ref[lo:hi, :] + mod_rows(gf_ref, lo, hi) * _rms(y, gpost_ref[...])

    if resident:
        prologue()
        project(0, bufs[0])
        for cc in range(1, nc):
            activate(cc - 1, bufs[(cc - 1) % 2])
            project(cc, bufs[cc % 2])
        activate(nc - 1, bufs[(nc - 1) % 2])
        epilogue()
        return

    c = pl.program_id(2)

    @pl.when(c == 0)
    def _():
        prologue()
        project(c, bufs[0])

    for parity in range(2):
        @pl.when((c > 0) & (c < nc) & (lax.rem(c, 2) == parity))
        def _():
            activate(c - 1, bufs[1 - parity])
            project(c, bufs[parity])

    @pl.when(c == nc)
    def _():
        activate(c - 1, bufs[(nc - 1) % 2])
        epilogue()


def _ffn(o, x, gate_m, shift, scale, gate_f, gpost_m, gpre, gpost, wo, w_up, cw, cb, wd, prev,
         *, tm, tf, halo, stride, mod_map, resident):
    nb, r, d = x.shape
    dff = wd[0].shape[-2]
    nc = dff // tf
    nt = r // tm
    mr = shift[0].shape[-2]
    slabs = (gate_m, shift, scale, gate_f, gpost_m, gpre, gpost, wo, w_up, cw, cb, wd)
    gate_m_a, shift_a, scale_a, gate_f_a, gpost_m_a, gpre_a, gpost_a, wo_a, w_up_a, cw_a, cb_a, wd_a = (
        s[0] for s in slabs)
    multi_tile = nt > 1
    rb = 64 if tm % 64 == 0 else tm
    once = pl.Buffered(1)
    row_mode = dict(pipeline_mode=once) if nb * nt == 1 else {}
    kern = functools.partial(_ffn_kernel, halo=halo, stride=stride, multi_tile=multi_tile, rb=rb, tf=tf,
                             resident=resident)
    scratch = [
        pltpu.VMEM((tm, d), F32), pltpu.VMEM((tm, d), BF16), pltpu.VMEM((nc, tm, tf), BF16),
        pltpu.VMEM((halo + tm, 2 * tf), F32), pltpu.VMEM((halo + tm, 2 * tf), F32),
        pltpu.VMEM((nc, halo, 2 * tf) if multi_tile else (1, 8, LANES), F32),
    ]
    out_sds = jax.ShapeDtypeStruct((nb, r, d), F32)
    if resident:
        im = lambda b, t: (b, t, 0)
        seq = lambda b, t: (b, 0, 0)
        const = lambda b, t: (0, 0)
        mod_spec = lambda p: _slab(p, (None, mr, d), lambda b, t: (mod_map(b), 0, 0))
        vec_spec = lambda p: _slab(p, (1, d), const)
        whole = lambda p, **kw: _slab(p, p[0].shape[-2:], const, **kw)
        out, tail = pl.pallas_call(
            kern, grid=(nb, nt),
            in_specs=[
                pl.BlockSpec((None, tm, Q_DIM), im, **row_mode), pl.BlockSpec((None, tm, d), im, **row_mode),
                mod_spec(gate_m), mod_spec(shift), mod_spec(scale), mod_spec(gate_f),
                vec_spec(gpost_m), vec_spec(gpre), vec_spec(gpost),
                whole(wo, pipeline_mode=once), whole(w_up, pipeline_mode=once), whole(cw), whole(cb),
                whole(wd, pipeline_mode=once),
                pl.BlockSpec((None, halo, 2 * dff), seq),
            ],
            out_specs=[pl.BlockSpec((None, tm, d), im), pl.BlockSpec((None, halo, 2 * dff), seq)],
            out_shape=[out_sds, jax.ShapeDtypeStruct((nb, halo, 2 * dff), F32)],
            scratch_shapes=scratch,
            compiler_params=_cparams(("arbitrary", "arbitrary")),
            name="mixer_out_conv_ffn",
        )(o, x, gate_m_a, shift_a, scale_a, gate_f_a, gpost_m_a, gpre_a, gpost_a, wo_a, w_up_a, cw_a, cb_a,
          wd_a, prev)
        return out, tail

    assert not multi_tile
    last = nc - 1
    im = lambda b, t, c: (b, t, 0)
    const = lambda b, t, c: (0, 0)
    mod_spec = lambda p: _slab(p, (None, mr, d), lambda b, t, c: (mod_map(b), 0, 0))
    vec_spec = lambda p: _slab(p, (1, d), const)
    whole = lambda p, **kw: _slab(p, p[0].shape[-2:], const, **kw)
    proj = lambda p, off: _slab(p, (p[0].shape[-2], tf), lambda b, t, c: (0, off + jnp.minimum(c, last)))
    actv = lambda p, off: _slab(p, (p[0].shape[-2], tf), lambda b, t, c: (0, off + jnp.maximum(c - 1, 0)))
    halo_spec = lambda off: pl.BlockSpec((None, halo, tf), lambda b, t, c: (b, 0, off + jnp.minimum(c, last)))
    tail_sds = jax.ShapeDtypeStruct((nb, halo, dff), F32)
    out, tail_g, tail_v = pl.pallas_call(
        kern, grid=(nb, nt, nc + 1),
        in_specs=[
            pl.BlockSpec((None, tm, Q_DIM), im, **row_mode), pl.BlockSpec((None, tm, d), im, **row_mode),
            mod_spec(gate_m), mod_spec(shift), mod_spec(scale), mod_spec(gate_f),
            vec_spec(gpost_m), vec_spec(gpre), vec_spec(gpost),
            whole(wo, pipeline_mode=once),
            proj(w_up, 0), proj(w_up, nc), actv(cw, 0), actv(cw, nc), actv(cb, 0), actv(cb, nc),
            whole(wd, pipeline_mode=once),
            halo_spec(0), halo_spec(nc),
        ],
        out_specs=[pl.BlockSpec((None, tm, d), im), halo_spec(0), halo_spec(0)],
        out_shape=[out_sds, tail_sds, tail_sds],
        scratch_shapes=scratch,
        compiler_params=_cparams(("arbitrary", "arbitrary", "arbitrary")),
        name="mixer_out_conv_ffn_chunked",
    )(o, x, gate_m_a, shift_a, scale_a, gate_f_a, gpost_m_a, gpre_a, gpost_a, wo_a, w_up_a, w_up_a,
      cw_a, cw_a, cb_a, cb_a, wd_a, prev, prev)
    return out, jnp.concatenate([tail_g, tail_v], axis=-1)


def _rope_tables(pos):
    inv_freq = jnp.power(jnp.float32(ROPE_THETA), -jnp.arange(ROT_HALF, dtype=F32) * (2.0 / ROT_DIM))
    ang = pos.astype(F32)[:, None] * inv_freq[None, :]
    cos, sin = jnp.cos(ang), jnp.sin(ang)
    n = pos.shape[0]
    ones = jnp.ones((n, HEAD_DIM - ROT_DIM), F32)
    zeros_r = jnp.zeros((n, HEAD_DIM - ROT_DIM), F32)
    zeros_h = jnp.zeros((n, ROT_HALF), F32)
    cq = jnp.concatenate([cos, cos, ones], axis=1)
    s1 = jnp.concatenate([zeros_h, sin, zeros_r], axis=1)
    s2 = jnp.concatenate([-sin, zeros_h, zeros_r], axis=1)
    rep = LANES // HEAD_DIM
    return (jnp.tile(cq, (1, rep)), jnp.tile(s1, (1, rep)), jnp.tile(s2, (1, rep)), cos.T, sin.T)


def _kv_onehot():
    oh = np.zeros((N_QBLK, 2, N_KV_HEADS), np.float32)
    for i in range(N_QBLK):
        for p in range(2):
            oh[i, p, 2 * (i // GROUP) + p] = 1.0
    return oh


def _expand_q(q_tm):
    dt, db, _ = q_tm.shape
    q5 = q_tm.reshape(dt, db, N_QBLK, 2, HEAD_DIM).transpose(1, 2, 3, 0, 4)
    oh = jnp.asarray(_kv_onehot(), q_tm.dtype)
    q6 = q5[:, :, :, :, None, :] * oh[None, :, :, None, :, None]
    return q6.reshape(db, N_QBLK * 2 * dt, KV_DIM)


def _select_o(o_full, dt):
    db = o_full.shape[0]
    o6 = o_full.reshape(db, N_QBLK, 2, dt, N_KV_HEADS, HEAD_DIM)
    o5 = jnp.sum(o6 * jnp.asarray(_kv_onehot())[None, :, :, None, :, None], axis=4)
    return o5.transpose(3, 0, 1, 2, 4).reshape(dt, db, Q_DIM).astype(BF16)


def _rows_of_heads(v, dt):
    vp = v[jnp.asarray(HEAD_PERM)]
    return jnp.broadcast_to(vp[:, None, None], (N_HEADS, dt, LANES)).reshape(N_HEADS * dt, LANES).astype(F32)


def _from_t(kt, lead):
    n = kt.shape[-1]
    x = kt.reshape(lead + (N_KV_HEADS, HEAD_DIM, n))
    nd = len(lead)
    return x.transpose(tuple(range(nd)) + (nd + 2, nd, nd + 1))


def kernel(x_prompt, x_sample, cache_sb_k, cache_sb_v, state_swa_k, state_swa_v, state_conv, page_table,
           c_prompt, c_sample, mod_w, mod_b, norm_mix_pre, norm_mix_post, norm_ffn_pre, norm_ffn_post,
           sb_w_qkv, sb_w_o, sb_bias, swa_w_qkv, swa_w_o, swa_sinks, ffn_w_up, ffn_conv_w, ffn_conv_b,
           ffn_w_down):
    b, t, d = x_prompt.shape
    db, dt, _ = x_sample.shape
    depth = mod_w.shape[0]
    n_pages = page_table.shape[1]
    past_len = n_pages * PAGE_SIZE
    wb = state_swa_k.shape[2]
    dff = ffn_w_down.shape[1]
    n_phys = cache_sb_k.shape[1]

    tm_p = min(512, t)
    tm_f = min(512, t)
    tf = 256

    mod = _modulation(jnp.concatenate([c_prompt, c_sample], axis=0), mod_w, mod_b)
    mod_p = mod[:, :, :b].reshape(depth, N_MOD, b, 1, d)
    mod_s = mod[:, :, b:].reshape(depth, N_MOD, 1, db, d)

    cache_kt = cache_sb_k.transpose(0, 1, 3, 4, 2).reshape(-1, n_phys, KV_DIM, PAGE_SIZE)
    cache_vt = cache_sb_v.transpose(0, 1, 3, 4, 2).reshape(-1, n_phys, KV_DIM, PAGE_SIZE)
    swa_kt = state_swa_k.transpose(0, 1, 3, 4, 2).reshape(-1, db, KV_DIM, wb)
    swa_vt = state_swa_v.transpose(0, 1, 3, 4, 2).reshape(-1, db, KV_DIM, wb)

    qcols = jnp.asarray(np.concatenate([np.arange(HEAD_DIM) + HEAD_DIM * h for h in HEAD_PERM]))
    perm = jnp.asarray(HEAD_PERM)

    def prep_mixer(w_qkv, w_o):
        return (w_qkv[:, :, :Q_DIM][:, :, qcols].astype(BF16),
                w_qkv[:, :, Q_DIM:].transpose(0, 2, 1).astype(BF16),
                w_o[:, qcols, :].astype(BF16))
    mixer_w = (prep_mixer(sb_w_qkv, sb_w_o), prep_mixer(swa_w_qkv, swa_w_o))
    w_up_all = ffn_w_up.astype(BF16)
    wd_all = ffn_w_down.astype(BF16)
    cb_all = ffn_conv_b.reshape(depth, 1, 2 * dff)
    gains = [g.reshape(depth, 1, d) for g in (norm_mix_pre, norm_mix_post, norm_ffn_pre, norm_ffn_post)]

    rope_p = _rope_tables(jnp.arange(t))
    rope_p = tuple(a[None] for a in rope_p) + (False,)
    cq, s1, s2, ck, sk = _rope_tables(past_len + jnp.arange(dt))
    rope_s = (cq[:, None, :], s1[:, None, :], s2[:, None, :],
              jnp.broadcast_to(ck.T[:, :, None], (dt, ROT_HALF, db)),
              jnp.broadcast_to(sk.T[:, :, None], (dt, ROT_HALF, db)), True)

    xp = x_prompt
    xs = x_sample.transpose(1, 0, 2)
    pmap = lambda bi: bi
    smap = lambda bi: 0

    outs = {k: [] for k in ("sbkp", "sbvp", "sbks", "sbvs", "swkp", "swvp", "swks", "swvs", "cp", "cs")}
    for i in range(depth):
        j = i // 2
        is_sb = i % 2 == 0
        wq, wkvt, wo = ((w, (j,)) for w in mixer_w[0 if is_sb else 1])
        g_pre, g_post, gf_pre, gf_post = ((g, (i,)) for g in gains)
        mp = [(mod_p, (i, k)) for k in range(N_MOD)]
        ms = [(mod_s, (i, k)) for k in range(N_MOD)]

        qp, ktp, vtp, kvbp = _qkv(xp, mp[0], mp[1], g_pre, wq, wkvt, None if is_sb else rope_p,
                                  tm=tm_p, mod_map=pmap)
        qs, kts, vts, _ = _qkv(xs, ms[0], ms[1], g_pre, wq, wkvt, None if is_sb else rope_s,
                               tm=db, mod_map=smap)
        qe = _expand_q(qs)
        kn = kts.transpose(2, 1, 0)
        vn = vts.transpose(2, 1, 0)
        if is_sb:
            op = _sb_prompt(qp, kvbp, sb_bias[j][perm])
            pad = ((0, 0), (0, 0), (0, LANES - dt))
            o_full = _sb_sample(page_table, qe, _rows_of_heads(sb_bias[j], dt),
                                jnp.pad(kn, pad).astype(BF16), jnp.pad(vn, pad).astype(BF16),
                                cache_kt, cache_vt, j, dt)
            outs["sbkp"].append(_from_t(ktp, (b,)))
            outs["sbvp"].append(_from_t(vtp, (b,)))
            outs["sbks"].append(_from_t(kn, (db,)))
            outs["sbvs"].append(_from_t(vn, (db,)))
        else:
            op = _swa_prompt(qp, kvbp, swa_sinks[j][perm])
            pad = ((0, 0), (0, 0), (wb - dt, 0))
            o_full, ko, vo = _swa_sample(qe, _rows_of_heads(swa_sinks[j], dt), swa_kt, swa_vt, j,
                                         jnp.pad(kn, pad), jnp.pad(vn, pad), dt, past_len)
            keep = min(WINDOW, t)
            outs["swkp"].append(_from_t(ktp[:, :, t - keep:], (b,)))
            outs["swvp"].append(_from_t(vtp[:, :, t - keep:], (b,)))
            outs["swks"].append(_from_t(ko, (db,)))
            outs["swvs"].append(_from_t(vo, (db,)))
        os_ = _select_o(o_full, dt)

        w_up, wd, cw, cb = (w_up_all, (i,)), (wd_all, (i,)), (ffn_conv_w, (i,)), (cb_all, (i,))

        halo_p = 8
        zero_prev = jnp.zeros((b, halo_p, 2 * dff), F32)
        xp, tail_p = _ffn(op, xp, mp[2], mp[3], mp[4], mp[5], g_post, gf_pre, gf_post, wo,
                          w_up, cw, cb, wd, zero_prev,
                          tm=tm_f, tf=tf, halo=halo_p, stride=1, mod_map=pmap, resident=True)
        st = state_conv[i].transpose(1, 0, 2).reshape(1, 2 * db, 2 * dff)
        xs1, tail_s = _ffn(os_.reshape(1, dt * db, Q_DIM), xs.reshape(1, dt * db, d),
                           ms[2], ms[3], ms[4], ms[5], g_post, gf_pre, gf_post, wo,
                           w_up, cw, cb, wd, st,
                           tm=dt * db, tf=tf, halo=2 * db, stride=db, mod_map=smap, resident=False)
        xs = xs1.reshape(dt, db, d)
        outs["cp"].append(tail_p[:, halo_p - 2:])
        outs["cs"].append(tail_s.reshape(2, db, 2 * dff).transpose(1, 0, 2))

    st_ = lambda k: jnp.stack(outs[k])
    return (xp, xs.transpose(1, 0, 2),
            st_("sbkp"), st_("sbvp"), st_("sbks"), st_("sbvs"),
            st_("swkp"), st_("swvp"), st_("swks"), st_("swvs"),
            st_("cp"), st_("cs"))
```
